```python
import jax, jax.numpy as jnp
from jax import lax
import numpy as np

D_MODEL = 4096
BATCH = 2
SEQ = 4096
DEPTH = 2

CHUNK = 64
D_MIX = D_MODEL
HEAD_DIM = 128
CONV_CH = D_MIX // 4
N_FOX_HEADS = (D_MIX - CONV_CH) // (2 * HEAD_DIM)
N_DN_HEADS = N_FOX_HEADS
FOX_W = N_FOX_HEADS * HEAD_DIM
DN_W = N_DN_HEADS * HEAD_DIM
CONF_KERNEL = 31
SHORT_CONV = 4
Q_BLOCK = 128
N_EXPERTS = 16
N_EXPERT_GROUPS = 4
EXPERTS_PER_GROUP = N_EXPERTS // N_EXPERT_GROUPS
TOP_K = 2
D_EXPERT = D_MODEL // 4
EXPERT_BLOCK = 128
EPS = 1e-6
IN_SPLITS = (CONV_CH, CONV_CH,
             FOX_W, FOX_W, FOX_W, N_FOX_HEADS, FOX_W,
             DN_W, DN_W, DN_W, N_DN_HEADS, N_DN_HEADS, DN_W)
N_IN = sum(IN_SPLITS)

kernel_name = 'hybrid_conv_fox_gdn_grouped_moe'


def rms_norm(x, g):
    xf = x.astype(jnp.float32)
    y = xf * lax.rsqrt(jnp.mean(xf * xf, axis=-1, keepdims=True) + EPS)
    return (y * g.astype(jnp.float32)).astype(x.dtype)


def layer_norm(x, g, b):
    xf = x.astype(jnp.float32)
    mu = jnp.mean(xf, axis=-1, keepdims=True)
    var = jnp.mean(jnp.square(xf - mu), axis=-1, keepdims=True)
    y = (xf - mu) * lax.rsqrt(var + EPS)
    return (y * g.astype(jnp.float32) + b.astype(jnp.float32)).astype(x.dtype)


def l2_norm(x):
    return x * lax.rsqrt(jnp.sum(x * x, axis=-1, keepdims=True) + EPS)


def causal_depthwise_conv(x, w):
    k_w, ch = w.shape
    xp = jnp.pad(x, ((0, 0), (k_w - 1, 0), (0, 0)))
    return lax.conv_general_dilated(xp, w[:, None, :].astype(x.dtype), window_strides=(1,),
                                    padding='VALID', dimension_numbers=('NWC', 'WIO', 'NWC'),
                                    feature_group_count=ch)


def conformer_conv(a_val, a_gate, conv_w, conv_b, ln_g, ln_b):
    a = a_val * jax.nn.sigmoid(a_gate)
    a = causal_depthwise_conv(a, conv_w) + conv_b
    a = layer_norm(a, ln_g, ln_b)
    return jax.nn.silu(a)


def forgetting_attention(q, k, v, f_logit, f_bias, qn_g, kn_g):
    _, seq, _, dh = q.shape
    q = rms_norm(q, qn_g).transpose(0, 2, 1, 3)
    k = rms_norm(k, kn_g).transpose(0, 2, 1, 3)
    v = v.transpose(0, 2, 1, 3)
    log_f = jax.nn.log_sigmoid(f_logit.astype(jnp.float32) + f_bias.astype(jnp.float32))
    cum = jnp.cumsum(log_f, axis=1).transpose(0, 2, 1)
    scale = dh ** -0.5
    outs = []
    for blk in range(seq // Q_BLOCK):
        q0, q1 = blk * Q_BLOCK, (blk + 1) * Q_BLOCK
        s = jnp.einsum('bhqd,bhkd->bhqk', q[:, :, q0:q1], k[:, :, :q1],
                       preferred_element_type=jnp.float32) * scale
        s = s + cum[:, :, q0:q1, None] - cum[:, :, None, :q1]
        qpos = jnp.arange(q0, q1)[:, None]
        kpos = jnp.arange(q1)[None, :]
        p = jax.nn.softmax(jnp.where(kpos <= qpos, s, -jnp.inf), axis=-1)
        outs.append(jnp.einsum('bhqk,bhkd->bhqd', p.astype(v.dtype), v[:, :, :q1]))
    return jnp.concatenate(outs, axis=2).transpose(0, 2, 1, 3)


def gated_delta_rule(q, k, v, g, beta):
    bsz, seq, nh, dk = q.shape
    dv = v.shape[-1]
    n_ch = seq // CHUNK
    f32 = jnp.float32
    q = l2_norm(q.astype(f32)) * (dk ** -0.5)
    k = l2_norm(k.astype(f32))

    def chunks(t):
        return t.astype(f32).reshape(bsz, n_ch, CHUNK, nh, -1).transpose(0, 3, 1, 2, 4)

    qc, kc, vc = chunks(q), chunks(k), chunks(v)
    gc = chunks(g[..., None])[..., 0]
    bc = chunks(beta[..., None])[..., 0]
    g_cum = jnp.cumsum(gc, axis=-1)
    idx = jnp.arange(CHUNK)
    causal = idx[:, None] >= idx[None, :]
    strict = idx[:, None] > idx[None, :]
    gamma = jnp.exp(jnp.where(causal, g_cum[..., :, None] - g_cum[..., None, :], -jnp.inf))
    kb = kc * bc[..., None]
    a_mat = jnp.where(strict, jnp.einsum('bhnid,bhnjd->bhnij', kb, kc) * gamma, 0.0)
    t_mat = a_mat + jnp.eye(CHUNK, dtype=f32)
    rhs = jnp.concatenate([vc * bc[..., None], kb * jnp.exp(g_cum)[..., None]], axis=-1)
    sol = lax.linalg.triangular_solve(t_mat, rhs, left_side=True, lower=True, unit_diagonal=True)
    u, w = sol[..., :dv], sol[..., dv:]
    qk = jnp.where(causal, jnp.einsum('bhnid,bhnjd->bhnij', qc, kc) * gamma, 0.0)
    g_last = g_cum[..., -1]
    q_dec = qc * jnp.exp(g_cum)[..., None]
    k_dec = kc * jnp.exp(g_last[..., None] - g_cum)[..., None]

    def step(state, xs):
        u_n, w_n, qk_n, q_n, k_n, gl_n = xs
        v_new = u_n - jnp.einsum('bhck,bhkv->bhcv', w_n, state)
        o_n = jnp.einsum('bhck,bhkv->bhcv', q_n, state) + jnp.einsum('bhij,bhjv->bhiv', qk_n, v_new)
        state = state * jnp.exp(gl_n)[..., None, None] + jnp.einsum('bhck,bhcv->bhkv', k_n, v_new)
        return state, o_n

    xs = tuple(jnp.moveaxis(t, 2, 0) for t in (u, w, qk, q_dec, k_dec, g_last))
    state0 = jnp.zeros((bsz, nh, dk, dv), f32)
    _, o = lax.scan(step, state0, xs)
    return o.transpose(1, 0, 3, 2, 4).reshape(bsz, seq, nh, dv).astype(v.dtype)


def token_mixer(h, w_in, conv_w, conv_b, conv_ln_g, conv_ln_b, fox_f_bias, fox_qn_g, fox_kn_g,
                fox_on_g, dn_conv_w, dn_a_log, dn_dt_bias, dn_on_g, w_out):
    bsz, seq, _ = h.shape
    proj = jnp.einsum('bsd,dn->bsn', h, w_in)
    cuts = [int(i) for i in np.cumsum(IN_SPLITS)[:-1]]
    (a_val, a_gate, fq, fk, fv, ff, fg, dq, dk, dv, db, da, dz) = jnp.split(proj, cuts, axis=-1)

    def heads(t):
        return t.reshape(bsz, seq, -1, HEAD_DIM)

    y_a = conformer_conv(a_val, a_gate, conv_w, conv_b, conv_ln_g, conv_ln_b)
    o_b = forgetting_attention(heads(fq), heads(fk), heads(fv), ff, fox_f_bias, fox_qn_g, fox_kn_g)
    y_b = (rms_norm(o_b, fox_on_g) * jax.nn.sigmoid(heads(fg))).reshape(bsz, seq, FOX_W)
    qkv = jax.nn.silu(causal_depthwise_conv(jnp.concatenate([dq, dk, dv], axis=-1), dn_conv_w))
    cq, ck, cv = jnp.split(qkv, [DN_W, 2 * DN_W], axis=-1)
    beta = jax.nn.sigmoid(db.astype(jnp.float32))
    g = -jnp.exp(dn_a_log.astype(jnp.float32)) * jax.nn.softplus(
        da.astype(jnp.float32) + dn_dt_bias.astype(jnp.float32))
    o_c = gated_delta_rule(heads(cq), heads(ck), heads(cv), g, beta)
    y_c = (rms_norm(o_c, dn_on_g) * jax.nn.silu(heads(dz))).reshape(bsz, seq, DN_W)
    y = jnp.concatenate([y_a, y_b, y_c], axis=-1)
    return jnp.einsum('bsm,md->bsd', y, w_out)


def moe_ffn(h, w_router, router_bias, w_gate, w_up, w_down):
    bsz, seq, d = h.shape
    n_tok = bsz * seq
    ht = h.reshape(n_tok, d)
    scores = jax.nn.sigmoid(jnp.einsum('td,de->te', ht, w_router, preferred_element_type=jnp.float32))
    sel = (scores + router_bias.astype(jnp.float32)).reshape(n_tok, N_EXPERT_GROUPS, EXPERTS_PER_GROUP)
    group_score = jnp.sum(lax.top_k(sel, 2)[0], axis=-1)
    best_group = jnp.argmax(group_score, axis=-1)
    in_group = jnp.take_along_axis(sel, best_group[:, None, None], axis=1)[:, 0]
    _, local = lax.top_k(in_group, TOP_K)
    expert_idx = best_group[:, None] * EXPERTS_PER_GROUP + local
    gate = jnp.take_along_axis(scores, expert_idx, axis=-1)
    gate = gate / jnp.sum(gate, axis=-1, keepdims=True)

    n_asg = n_tok * TOP_K
    flat_e = expert_idx.reshape(n_asg)
    order = jnp.argsort(flat_e)
    sorted_e = flat_e[order]
    counts = jnp.bincount(flat_e, length=N_EXPERTS)
    padded = (counts + EXPERT_BLOCK - 1) // EXPERT_BLOCK * EXPERT_BLOCK
    seg_start = jnp.cumsum(counts) - counts
    pad_end = jnp.cumsum(padded)
    pad_start = pad_end - padded
    dest_sorted = (pad_start[sorted_e] + jnp.arange(n_asg) - seg_start[sorted_e]).astype(jnp.int32)
    n_rows = n_asg + N_EXPERTS * EXPERT_BLOCK
    n_blocks = n_rows // EXPERT_BLOCK
    row_token = jnp.zeros((n_rows,), jnp.int32).at[dest_sorted].set((order // TOP_K).astype(jnp.int32))
    block_expert = jnp.minimum(
        jnp.searchsorted(pad_end, jnp.arange(n_blocks) * EXPERT_BLOCK, side='right'), N_EXPERTS - 1)
    rows = ht[row_token].reshape(n_blocks, EXPERT_BLOCK, d)

    def expert_block(args):
        xb, e = args
        return (jax.nn.silu(xb @ w_gate[e]) * (xb @ w_up[e])) @ w_down[e]

    out_rows = lax.map(expert_block, (rows, block_expert)).reshape(n_rows, d)
    dest = jnp.zeros((n_asg,), jnp.int32).at[order].set(dest_sorted)
    y = out_rows[dest].reshape(n_tok, TOP_K, d)
    y = jnp.einsum('tkd,tk->td', y, gate.astype(y.dtype))
    return y.reshape(bsz, seq, d)


def setup_inputs(seed: int = 0) -> dict:
    key = jax.random.key(seed)
    ks = jax.random.split(key, 28)
    f32 = jnp.float32
    L, D = DEPTH, D_MODEL

    def nrm(k, shape, scale):
        return jax.random.normal(k, shape, f32) * scale

    dt = jnp.exp(jax.random.uniform(ks[17], (L, N_DN_HEADS), f32, np.log(1e-3), np.log(1e-1)))
    return {
        'x': nrm(ks[0], (BATCH, SEQ, D), 1.0),
        'c': nrm(ks[1], (BATCH, D), 1.0),
        'ada_w': nrm(ks[2], (L, D, 6 * D), 0.5 * D ** -0.5),
        'ada_b': nrm(ks[3], (L, 6 * D), 0.02),
        'norm_mix_g': 1.0 + nrm(ks[4], (L, D), 0.02),
        'norm_ffn_g': 1.0 + nrm(ks[5], (L, D), 0.02),
        'w_in': nrm(ks[6], (L, D, N_IN), D ** -0.5),
        'conv_w': nrm(ks[7], (L, CONF_KERNEL, CONV_CH), CONF_KERNEL ** -0.5),
        'conv_b': nrm(ks[8], (L, CONV_CH), 0.02),
        'conv_ln_g': 1.0 + nrm(ks[9], (L, CONV_CH), 0.02),
        'conv_ln_b': nrm(ks[10], (L, CONV_CH), 0.02),
        'fox_f_bias': jax.random.uniform(ks[11], (L, N_FOX_HEADS), f32, 2.0, 5.0),
        'fox_qn_g': 1.0 + nrm(ks[12], (L, HEAD_DIM), 0.02),
        'fox_kn_g': 1.0 + nrm(ks[13], (L, HEAD_DIM), 0.02),
        'fox_on_g': 1.0 + nrm(ks[14], (L, HEAD_DIM), 0.02),
        'dn_conv_w': nrm(ks[15], (L, SHORT_CONV, 3 * DN_W), SHORT_CONV ** -0.5),
        'dn_a_log': jnp.log(jax.random.uniform(ks[16], (L, N_DN_HEADS), f32, 1.0, 16.0)),
        'dn_dt_bias': dt + jnp.log(-jnp.expm1(-dt)),
        'dn_on_g': 1.0 + nrm(ks[18], (L, HEAD_DIM), 0.02),
        'w_out': nrm(ks[19], (L, D_MIX, D), D_MIX ** -0.5),
        'w_router': nrm(ks[20], (D, N_EXPERTS), D ** -0.5),
        'router_bias': nrm(ks[21], (N_EXPERTS,), 0.01),
        'w_gate_e': nrm(ks[22], (L, N_EXPERTS, D, D_EXPERT), D ** -0.5),
        'w_up_e': nrm(ks[23], (L, N_EXPERTS, D, D_EXPERT), D ** -0.5),
        'w_down_e': nrm(ks[24], (L, N_EXPERTS, D_EXPERT, D), D_EXPERT ** -0.5),
        'final_g': 1.0 + nrm(ks[25], (D,), 0.02),
    }


def reference(x, c, ada_w, ada_b, norm_mix_g, norm_ffn_g, w_in, conv_w, conv_b, conv_ln_g,
              conv_ln_b, fox_f_bias, fox_qn_g, fox_kn_g, fox_on_g, dn_conv_w, dn_a_log, dn_dt_bias,
              dn_on_g, w_out, w_router, router_bias, w_gate_e, w_up_e, w_down_e, final_g):
    c_act = jax.nn.silu(c)
    for l in range(DEPTH):
        mod = (jnp.einsum('bd,de->be', c_act, ada_w[l]) + ada_b[l])[:, None, :]
        shift_m, scale_m, gate_m, shift_f, scale_f, gate_f = jnp.split(mod, 6, axis=-1)
        h = rms_norm(x, norm_mix_g[l]) * (1.0 + scale_m) + shift_m
        y = token_mixer(h, w_in[l], conv_w[l], conv_b[l], conv_ln_g[l], conv_ln_b[l], fox_f_bias[l],
                        fox_qn_g[l], fox_kn_g[l], fox_on_g[l], dn_conv_w[l], dn_a_log[l],
                        dn_dt_bias[l], dn_on_g[l], w_out[l])
        x = x + gate_m * y
        h = rms_norm(x, norm_ffn_g[l]) * (1.0 + scale_f) + shift_f
        y = moe_ffn(h, w_router, router_bias, w_gate_e[l], w_up_e[l], w_down_e[l])
        x = x + gate_f * y
    return rms_norm(x, final_g)
```

```python
import functools

import jax
import jax.numpy as jnp
from jax import lax
from jax.experimental import pallas as pl
from jax.experimental.pallas import tpu as pltpu

F32 = jnp.float32
BF16 = jnp.bfloat16

HEAD_DIM = 128
N_HEADS = 12
HEADS_W = N_HEADS * HEAD_DIM
CONV_CH = 1024
CONF_KERNEL = 31
SHORT_CONV = 4
GDN_CHUNK = 64
N_EXPERTS = 16
N_EXPERT_GROUPS = 4
EXPERTS_PER_GROUP = 4
TOP_K = 2
D_EXPERT = 1024
EPS = 1e-6
LANES = 128
VMEM_LIMIT = 56 * 1024 * 1024

COL_FQ = 0
COL_FK = COL_FQ + HEADS_W
COL_FV = COL_FK + HEADS_W
COL_FG = COL_FV + HEADS_W
COL_DQ = COL_FG + HEADS_W
COL_DK = COL_DQ + HEADS_W
COL_DV = COL_DK + HEADS_W
COL_DZ = COL_DV + HEADS_W
COL_AVAL = COL_DZ + HEADS_W
COL_AGATE = COL_AVAL + CONV_CH
N_BIG = COL_AGATE + CONV_CH
LANE_FF = 0
LANE_DB = N_HEADS
LANE_DA = 2 * N_HEADS


def _params(sem, vmem=VMEM_LIMIT):
    return pltpu.CompilerParams(dimension_semantics=sem, vmem_limit_bytes=vmem)


def _sigmoid(x):
    return 1.0 / (1.0 + jnp.exp(-x))


def _silu(x):
    return x * _sigmoid(x)


def _adaln_kernel(c_ref, w_ref, b_ref, o_ref):
    c = c_ref[...]
    ca = _silu(c).astype(BF16)
    o_ref[0] = jnp.dot(ca, w_ref[0].astype(BF16), preferred_element_type=F32) + b_ref[0]


def adaln(c, ada_w, ada_b, tn=1024):
    depth, d, n = ada_w.shape
    bsz = c.shape[0]
    cp = jnp.zeros((8, d), F32).at[:bsz].set(c)
    out = pl.pallas_call(
        _adaln_kernel,
        grid=(depth, n // tn),
        in_specs=[
            pl.BlockSpec((8, d), lambda l, j: (0, 0)),
            pl.BlockSpec((1, d, tn), lambda l, j: (l, 0, j)),
            pl.BlockSpec((1, 1, tn), lambda l, j: (l, 0, j)),
        ],
        out_specs=pl.BlockSpec((1, 8, tn), lambda l, j: (l, 0, j)),
        out_shape=jax.ShapeDtypeStruct((depth, 8, n), F32),
        compiler_params=_params(("arbitrary", "arbitrary")),
        name="adaln",
    )(cp, ada_w, ada_b.reshape(depth, 1, n))
    return out[:, :bsz]


def _norm_mod_kernel(x_ref, g_ref, sc_ref, sh_ref, o_ref):
    x = x_ref[...]
    ms = jnp.mean(x * x, axis=-1, keepdims=True)
    y = x * lax.rsqrt(ms + EPS) * g_ref[...]
    o_ref[...] = (y * (1.0 + sc_ref[0]) + sh_ref[0]).astype(o_ref.dtype)


def _norm_mod_router_kernel(x_ref, g_ref, sc_ref, sh_ref, wr_ref, o_ref, r_ref):
    x = x_ref[...]
    ms = jnp.mean(x * x, axis=-1, keepdims=True)
    y = x * lax.rsqrt(ms + EPS) * g_ref[...]
    h = y * (1.0 + sc_ref[0]) + sh_ref[0]
    o_ref[...] = h.astype(o_ref.dtype)
    r_ref[...] = jnp.dot(h, wr_ref[...], preferred_element_type=F32, precision=lax.Precision.HIGHEST)


def norm_mod(x2, g, scale, shift, seq, w_router=None, tm=256):
    t, d = x2.shape
    bsz = scale.shape[0]
    per_b = seq // tm
    in_specs = [
        pl.BlockSpec((tm, d), lambda i: (i, 0)),
        pl.BlockSpec((1, d), lambda i: (0, 0)),
        pl.BlockSpec((1, 1, d), lambda i: (i // per_b, 0, 0)),
        pl.BlockSpec((1, 1, d), lambda i: (i // per_b, 0, 0)),
    ]
    args = [x2, g.reshape(1, d), scale.reshape(bsz, 1, d), shift.reshape(bsz, 1, d)]
    if w_router is None:
        return pl.pallas_call(
            _norm_mod_kernel,
            grid=(t // tm,),
            in_specs=in_specs,
            out_specs=pl.BlockSpec((tm, d), lambda i: (i, 0)),
            out_shape=jax.ShapeDtypeStruct((t, d), BF16),
            compiler_params=_params(("arbitrary",)),
            name="norm_mod",
        )(*args)
    n_e = w_router.shape[1]
    wr = jnp.zeros((d, LANES), F32).at[:, :n_e].set(w_router)
    return pl.pallas_call(
        _norm_mod_router_kernel,
        grid=(t // tm,),
        in_specs=in_specs + [pl.BlockSpec((d, LANES), lambda i: (0, 0))],
        out_specs=[pl.BlockSpec((tm, d), lambda i: (i, 0)), pl.BlockSpec((tm, LANES), lambda i: (i, 0))],
        out_shape=[jax.ShapeDtypeStruct((t, d), F32), jax.ShapeDtypeStruct((t, LANES), F32)],
        compiler_params=_params(("arbitrary",)),
        name="norm_mod_router",
    )(*args, wr)


def _matmul_kernel(a_ref, b_ref, o_ref):
    o_ref[...] = jnp.dot(a_ref[...], b_ref[...], preferred_element_type=F32).astype(o_ref.dtype)


def matmul(a, b, out_dtype, tm=512, tn=1024):
    m, k = a.shape
    n = b.shape[1]
    tn = min(tn, n)
    return pl.pallas_call(
        _matmul_kernel,
        grid=(n // tn, m // tm),
        in_specs=[pl.BlockSpec((tm, k), lambda j, i: (i, 0)), pl.BlockSpec((k, tn), lambda j, i: (0, j))],
        out_specs=pl.BlockSpec((tm, tn), lambda j, i: (i, j)),
        out_shape=jax.ShapeDtypeStruct((m, n), out_dtype),
        compiler_params=_params(("arbitrary", "arbitrary")),
        name="matmul",
    )(a, b)


def _out_proj_kernel(ya_ref, yb_ref, yc_ref, wa_ref, wb_ref, wc_ref, x_ref, gate_ref, o_ref):
    acc = jnp.dot(ya_ref[...], wa_ref[...], preferred_element_type=F32)
    acc += jnp.dot(yb_ref[...], wb_ref[...], preferred_element_type=F32)
    acc += jnp.dot(yc_ref[...], wc_ref[...], preferred_element_type=F32)
    o_ref[...] = x_ref[...] + gate_ref[0] * acc


def out_proj(ya, yb, yc, w_out_bf, x2, gate, seq, tm=512, tn=1024):
    t, d = x2.shape
    bsz = gate.shape[0]
    per_b = seq // tm
    ka, kb, kc = ya.shape[1], yb.shape[1], yc.shape[1]
    wa, wb, wc = w_out_bf[:ka], w_out_bf[ka:ka + kb], w_out_bf[ka + kb:]
    return pl.pallas_call(
        _out_proj_kernel,
        grid=(d // tn, t // tm),
        in_specs=[
            pl.BlockSpec((tm, ka), lambda j, i: (i, 0)),
            pl.BlockSpec((tm, kb), lambda j, i: (i, 0)),
            pl.BlockSpec((tm, kc), lambda j, i: (i, 0)),
            pl.BlockSpec((ka, tn), lambda j, i: (0, j)),
            pl.BlockSpec((kb, tn), lambda j, i: (0, j)),
            pl.BlockSpec((kc, tn), lambda j, i: (0, j)),
            pl.BlockSpec((tm, tn), lambda j, i: (i, j)),
            pl.BlockSpec((1, 1, tn), lambda j, i: (i // per_b, 0, j)),
        ],
        out_specs=pl.BlockSpec((tm, tn), lambda j, i: (i, j)),
        out_shape=jax.ShapeDtypeStruct((t, d), F32),
        compiler_params=_params(("arbitrary", "arbitrary")),
        name="out_proj",
    )(ya, yb, yc, wa, wb, wc, x2, gate.reshape(bsz, 1, d))


CONV_HALO = 32


def _conv_module_kernel(val_ref, gate_ref, w_ref, b_ref, g_ref, beta_ref, o_ref, buf_ref, acc_ref, *, tt):
    i = pl.program_id(1)

    @pl.when(i == 0)
    def _():
        buf_ref[0:CONV_HALO, :] = jnp.zeros((CONV_HALO, CONV_CH), F32)

    a = val_ref[...].astype(F32) * _sigmoid(gate_ref[...].astype(F32))
    buf_ref[CONV_HALO:CONV_HALO + tt, :] = a

    rc = 64
    off = CONV_HALO - (CONF_KERNEL - 1)

    def col_body(c, carry):
        c0 = pl.multiple_of(c * LANES, LANES)
        w = w_ref[:, pl.ds(c0, LANES)]
        bias = b_ref[:, pl.ds(c0, LANES)]
        for r in range(tt // rc):
            acc = jnp.broadcast_to(bias, (rc, LANES))
            for k in range(CONF_KERNEL):
                acc = acc + w[k:k + 1, :] * buf_ref[r * rc + off + k:r * rc + off + k + rc, pl.ds(c0, LANES)]
            acc_ref[r * rc:(r + 1) * rc, pl.ds(c0, LANES)] = acc
        return carry

    lax.fori_loop(0, CONV_CH // LANES, col_body, 0)

    buf_ref[0:CONV_HALO, :] = buf_ref[tt:tt + CONV_HALO, :]

    y = acc_ref[...]
    mu = jnp.mean(y, axis=-1, keepdims=True)
    yc = y - mu
    var = jnp.mean(yc * yc, axis=-1, keepdims=True)
    z = yc * lax.rsqrt(var + EPS) * g_ref[...] + beta_ref[...]
    o_ref[...] = _silu(z).astype(o_ref.dtype)


def conv_module(proj, conv_w, conv_b, ln_g, ln_b, bsz, seq, tt=256):
    t = proj.shape[0]
    per_b = seq // tt
    return pl.pallas_call(
        functools.partial(_conv_module_kernel, tt=tt),
        grid=(bsz, per_b),
        in_specs=[
            pl.BlockSpec((tt, CONV_CH), lambda b, i: (b * per_b + i, COL_AVAL // CONV_CH)),
            pl.BlockSpec((tt, CONV_CH), lambda b, i: (b * per_b + i, COL_AGATE // CONV_CH)),
            pl.BlockSpec((CONF_KERNEL, CONV_CH), lambda b, i: (0, 0)),
            pl.BlockSpec((1, CONV_CH), lambda b, i: (0, 0)),
            pl.BlockSpec((1, CONV_CH), lambda b, i: (0, 0)),
            pl.BlockSpec((1, CONV_CH), lambda b, i: (0, 0)),
        ],
        out_specs=pl.BlockSpec((tt, CONV_CH), lambda b, i: (b * per_b + i, 0)),
        out_shape=jax.ShapeDtypeStruct((t, CONV_CH), BF16),
        scratch_shapes=[pltpu.VMEM((tt + CONV_HALO, CONV_CH), F32), pltpu.VMEM((tt, CONV_CH), F32)],
        compiler_params=_params(("arbitrary", "arbitrary")),
        name="conv_module",
    )(proj, proj, conv_w, conv_b.reshape(1, -1), ln_g.reshape(1, -1), ln_b.reshape(1, -1))


def _softplus(x):
    return jnp.maximum(x, 0.0) + jnp.log(1.0 + jnp.exp(-jnp.abs(x)))


def _gates_kernel(s_ref, fb_ref, al_ref, dtb_ref, col_ref, row_ref, carry_ref, *, tt):
    i = pl.program_id(1)

    @pl.when(i == 0)
    def _():
        carry_ref[...] = jnp.zeros_like(carry_ref)

    s = s_ref[...]
    lane = lax.broadcasted_iota(jnp.int32, (tt, LANES), 1)
    is_f = lane < LANE_DB
    is_b = jnp.logical_and(lane >= LANE_DB, lane < LANE_DA)
    is_g = jnp.logical_and(lane >= LANE_DA, lane < LANE_DA + N_HEADS)
    log_f = -_softplus(-(s + fb_ref[...]))
    beta = _sigmoid(s)
    g = -jnp.exp(al_ref[...]) * _softplus(s + dtb_ref[...])
    vals = jnp.where(is_f, log_f, jnp.where(is_g, g, 0.0))
    r = lax.broadcasted_iota(jnp.int32, (tt, tt), 0)
    c = lax.broadcasted_iota(jnp.int32, (tt, tt), 1)
    tri = r >= c
    l_full = jnp.where(tri, 1.0, 0.0).astype(F32)
    l_blk = jnp.where(jnp.logical_and(tri, (r // GDN_CHUNK) == (c // GDN_CHUNK)), 1.0, 0.0).astype(F32)
    hp = lax.Precision.HIGHEST
    full = jnp.dot(l_full, vals, preferred_element_type=F32, precision=hp) + carry_ref[0:1, :]
    blk = jnp.dot(l_blk, vals, preferred_element_type=F32, precision=hp)
    carry_ref[0:1, :] = full[tt - 1:tt, :]
    out = jnp.where(is_f, full, jnp.where(is_b, beta, jnp.where(is_g, blk, 0.0)))
    col_ref[...] = out
    row_ref[0] = out.T


def gates_prep(small, f_bias, a_log, dt_bias, bsz, seq, tt=256):
    t = small.shape[0]
    per_b = seq // tt
    fb = jnp.zeros((1, LANES), F32).at[0, LANE_FF:LANE_FF + N_HEADS].set(f_bias)
    al = jnp.zeros((1, LANES), F32).at[0, LANE_DA:LANE_DA + N_HEADS].set(a_log)
    dtb = jnp.zeros((1, LANES), F32).at[0, LANE_DA:LANE_DA + N_HEADS].set(dt_bias)
    vec = pl.BlockSpec((1, LANES), lambda b, i: (0, 0))
    return pl.pallas_call(
        functools.partial(_gates_kernel, tt=tt),
        grid=(bsz, per_b),
        in_specs=[pl.BlockSpec((tt, LANES), lambda b, i: (b * per_b + i, 0)), vec, vec, vec],
        out_specs=[pl.BlockSpec((tt, LANES), lambda b, i: (b * per_b + i, 0)),
                   pl.BlockSpec((1, LANES, tt), lambda b, i: (b, 0, i))],
        out_shape=[jax.ShapeDtypeStruct((t, LANES), F32), jax.ShapeDtypeStruct((bsz, LANES, seq), F32)],
        scratch_shapes=[pltpu.VMEM((8, LANES), F32)],
        compiler_params=_params(("arbitrary", "arbitrary")),
        name="gates_prep",
    )(small, fb, al, dtb)


def _lane_col(x, idx):
    lane = lax.broadcasted_iota(jnp.int32, x.shape, 1)
    return jnp.sum(jnp.where(lane == idx, x, 0.0), axis=1, keepdims=True)


def _fox_kernel(q_ref, k_ref, v_ref, fg_ref, ccol_ref, crow_ref, qg_ref, kg_ref, og_ref, o_ref, kn_ref,
                *, tq, seq):
    h = pl.program_id(1)
    i = pl.program_id(2)
    tk = tq

    @pl.when(i == 0)
    def _():
        def kbody(c, carry):
            r0 = pl.multiple_of(c * tk, tk)
            kk = k_ref[pl.ds(r0, tk), :].astype(F32)
            ms = jnp.mean(kk * kk, axis=-1, keepdims=True)
            kn_ref[pl.ds(r0, tk), :] = (kk * lax.rsqrt(ms + EPS) * kg_ref[...]).astype(BF16)
            return carry
        lax.fori_loop(0, seq // tk, kbody, 0)

    q = q_ref[...].astype(F32)
    qms = jnp.mean(q * q, axis=-1, keepdims=True)
    q = (q * lax.rsqrt(qms + EPS) * qg_ref[...] * (HEAD_DIM ** -0.5)).astype(BF16)
    fq = _lane_col(ccol_ref[...], LANE_FF + h)

    def step(j, carry, masked):
        m, l, acc = carry
        r0 = pl.multiple_of(j * tk, tk)
        kc = kn_ref[pl.ds(r0, tk), :]
        s = lax.dot_general(q, kc, (((1,), (1,)), ((), ())), preferred_element_type=F32)
        fk = crow_ref[0, pl.ds(LANE_FF + h, 1), pl.ds(r0, tk)]
        s = s + fq - fk
        if masked:
            row = lax.broadcasted_iota(jnp.int32, (tq, tk), 0)
            col = lax.broadcasted_iota(jnp.int32, (tq, tk), 1)
            s = jnp.where(col <= row, s, -jnp.inf)
        m_new = jnp.maximum(m, jnp.max(s, axis=1, keepdims=True))
        alpha = jnp.exp(m - m_new)
        p = jnp.exp(s - m_new)
        l = alpha * l + jnp.sum(p, axis=1, keepdims=True)
        acc = alpha * acc + jnp.dot(p.astype(BF16), v_ref[pl.ds(r0, tk), :], preferred_element_type=F32)
        return m_new, l, acc

    init = (jnp.full((tq, 1), -jnp.inf, F32), jnp.zeros((tq, 1), F32), jnp.zeros((tq, HEAD_DIM), F32))
    carry = lax.fori_loop(0, i, lambda j, c: step(j, c, False), init)
    _, l, acc = step(i, carry, True)
    o = acc / l
    oms = jnp.mean(o * o, axis=-1, keepdims=True)
    o = o * lax.rsqrt(oms + EPS) * og_ref[...]
    o_ref[...] = (o * _sigmoid(fg_ref[...].astype(F32))).astype(o_ref.dtype)


def fox_attention(proj, gates_col, gates_row, qn_g, kn_g, on_g, bsz, seq, tq=512):
    t = proj.shape[0]
    nq = seq // tq
    cb = lambda col: col // HEAD_DIM
    vec = pl.BlockSpec((1, HEAD_DIM), lambda b, h, i: (0, 0))
    return pl.pallas_call(
        functools.partial(_fox_kernel, tq=tq, seq=seq),
        grid=(bsz, N_HEADS, nq),
        in_specs=[
            pl.BlockSpec((tq, HEAD_DIM), lambda b, h, i: (b * nq + i, cb(COL_FQ) + h)),
            pl.BlockSpec((seq, HEAD_DIM), lambda b, h, i: (b, cb(COL_FK) + h)),
            pl.BlockSpec((seq, HEAD_DIM), lambda b, h, i: (b, cb(COL_FV) + h)),
            pl.BlockSpec((tq, HEAD_DIM), lambda b, h, i: (b * nq + i, cb(COL_FG) + h)),
            pl.BlockSpec((tq, LANES), lambda b, h, i: (b * nq + i, 0)),
            pl.BlockSpec((1, 16, seq), lambda b, h, i: (b, 0, 0)),
            vec, vec, vec,
        ],
        out_specs=pl.BlockSpec((tq, HEAD_DIM), lambda b, h, i: (b * nq + i, h)),
        out_shape=jax.ShapeDtypeStruct((t, HEADS_W), BF16),
        scratch_shapes=[pltpu.VMEM((seq, HEAD_DIM), BF16)],
        compiler_params=_params(("arbitrary", "arbitrary", "arbitrary")),
        name="fox_attention",
    )(proj, proj, proj, proj, gates_col, gates_row,
      qn_g.reshape(1, -1), kn_g.reshape(1, -1), on_g.reshape(1, -1))


DN_HALO = 8


def _gdn_prep_kernel(q_ref, k_ref, v_ref, w_ref, qo_ref, ko_ref, vo_ref, buf_ref, hist_ref, *, tt):
    i = pl.program_id(1)

    @pl.when(i == 0)
    def _():
        hist_ref[...] = jnp.zeros_like(hist_ref)

    off = DN_HALO - (SHORT_CONV - 1)
    for part, (x_ref, o_ref) in enumerate(((q_ref, qo_ref), (k_ref, ko_ref), (v_ref, vo_ref))):
        buf_ref[0:DN_HALO, :] = hist_ref[part]
        buf_ref[DN_HALO:DN_HALO + tt, :] = x_ref[...].astype(F32)
        hist_ref[part] = buf_ref[tt:tt + DN_HALO, :]
        w = w_ref[:, part * HEADS_W:(part + 1) * HEADS_W]
        y = w[0:1, :] * buf_ref[off:off + tt, :]
        for k in range(1, SHORT_CONV):
            y = y + w[k:k + 1, :] * buf_ref[off + k:off + k + tt, :]
        y = _silu(y)
        if part < 2:
            for hh in range(N_HEADS):
                yh = y[:, hh * HEAD_DIM:(hh + 1) * HEAD_DIM]
                nrm = lax.rsqrt(jnp.sum(yh * yh, axis=-1, keepdims=True) + EPS)
                if part == 0:
                    nrm = nrm * (HEAD_DIM ** -0.5)
                o_ref[:, hh * HEAD_DIM:(hh + 1) * HEAD_DIM] = (yh * nrm).astype(o_ref.dtype)
        else:
            o_ref[...] = y.astype(o_ref.dtype)


def gdn_prep(proj, dn_conv_w, bsz, seq, tt=256):
    t = proj.shape[0]
    per_b = seq // tt
    blk = lambda col: pl.BlockSpec((tt, HEADS_W), lambda b, i: (b * per_b + i, col // HEADS_W))
    out_spec = pl.BlockSpec((tt, HEADS_W), lambda b, i: (b * per_b + i, 0))
    return pl.pallas_call(
        functools.partial(_gdn_prep_kernel, tt=tt),
        grid=(bsz, per_b),
        in_specs=[blk(COL_DQ), blk(COL_DK), blk(COL_DV),
                  pl.BlockSpec((SHORT_CONV, 3 * HEADS_W), lambda b, i: (0, 0))],
        out_specs=[out_spec, out_spec, out_spec],
        out_shape=[jax.ShapeDtypeStruct((t, HEADS_W), BF16)] * 3,
        scratch_shapes=[pltpu.VMEM((tt + DN_HALO, HEADS_W), F32), pltpu.VMEM((3, DN_HALO, HEADS_W), F32)],
        compiler_params=_params(("arbitrary", "arbitrary")),
        name="gdn_prep",
    )(proj, proj, proj, dn_conv_w)


GDN_GROUP = 4 * GDN_CHUNK


def _gdn_intra_kernel(q_ref, k_ref, v_ref, gcol_ref, grow_ref, u_ref, w_ref, qd_ref, kd_ref, qk_ref, gl_ref,
                      *, tt):
    h = pl.program_id(1)
    n = GDN_GROUP
    row = lax.broadcasted_iota(jnp.int32, (n, n), 0)
    col = lax.broadcasted_iota(jnp.int32, (n, n), 1)
    same = (row // GDN_CHUNK) == (col // GDN_CHUNK)
    causal = jnp.logical_and(same, row >= col)
    strict = jnp.logical_and(same, row > col)
    eye = jnp.where(row == col, 1.0, 0.0).astype(F32)
    sub = (LANE_DA + h) % 8
    nt = (((1,), (1,)), ((), ()))
    for g in range(tt // n):
        rows = slice(g * n, (g + 1) * n)
        kc = k_ref[rows, :]
        qc = q_ref[rows, :]
        vc = v_ref[rows, :].astype(F32)
        gcol = gcol_ref[rows, :]
        beta = _lane_col(gcol, LANE_DB + h)
        gcc = _lane_col(gcol, LANE_DA + h)
        gcr = grow_ref[0, pl.ds(sub, 1), rows]
        gamma = jnp.exp(jnp.where(causal, gcc - gcr, -jnp.inf))
        kf = kc.astype(F32)
        kb = kf * beta
        kk = lax.dot_general(kb.astype(BF16), kc, nt, preferred_element_type=F32)
        x = -jnp.where(strict, kk * gamma, 0.0)
        p = eye + x
        x16 = x.astype(BF16)
        x = jnp.dot(x16, x16, preferred_element_type=F32)
        for _ in range(4):
            x16 = x.astype(BF16)
            xp = jnp.dot(jnp.concatenate([x16, p.astype(BF16)], axis=0), x16, preferred_element_type=F32)
            x = xp[:n]
            p = p + xp[n:]
        p = p + jnp.dot(p.astype(BF16), x.astype(BF16), preferred_element_type=F32)
        eg = jnp.exp(gcc)
        rhs = jnp.concatenate([vc * beta, kb * eg], axis=1).astype(BF16)
        sol = jnp.dot(p.astype(BF16), rhs, preferred_element_type=F32)
        u_ref[rows, :] = sol[:, :HEAD_DIM]
        w_ref[rows, :] = sol[:, HEAD_DIM:].astype(w_ref.dtype)
        qk = jnp.where(causal, lax.dot_general(qc, kc, nt, preferred_element_type=F32) * gamma, 0.0)
        glast = jnp.concatenate(
            [jnp.broadcast_to(gcc[(c + 1) * GDN_CHUNK - 1:(c + 1) * GDN_CHUNK, :], (GDN_CHUNK, 1))
             for c in range(n // GDN_CHUNK)], axis=0)
        qd_ref[rows, :] = (qc.astype(F32) * eg).astype(qd_ref.dtype)
        kd_ref[rows, :] = (kf * jnp.exp(glast - gcc)).astype(kd_ref.dtype)
        for c in range(n // GDN_CHUNK):
            cs = slice(c * GDN_CHUNK, (c + 1) * GDN_CHUNK)
            qk_ref[0, 0, g * n + c * GDN_CHUNK:g * n + (c + 1) * GDN_CHUNK, :] = qk[cs, cs].astype(qk_ref.dtype)
            ci = g * (n // GDN_CHUNK) + c
            gl_ref[0, 0, ci:ci + 1, :] = jnp.broadcast_to(jnp.exp(glast[c * GDN_CHUNK:c * GDN_CHUNK + 1, :]),
                                                          (1, LANES))


def gdn_intra(qn, kn, vn, gates_col, gates_row, bsz, seq, tt=512):
    t = qn.shape[0]
    per_b = seq // tt
    hb = pl.BlockSpec((tt, HEAD_DIM), lambda b, h, i: (b * per_b + i, h))
    return pl.pallas_call(
        functools.partial(_gdn_intra_kernel, tt=tt),
        grid=(bsz, N_HEADS, per_b),
        in_specs=[hb, hb, hb,
                  pl.BlockSpec((tt, LANES), lambda b, h, i: (b * per_b + i, 0)),
                  pl.BlockSpec((1, 8, tt), lambda b, h, i: (b, (LANE_DA + h) // 8, i))],
        out_specs=[hb, hb, hb, hb,
                   pl.BlockSpec((1, 1, tt, GDN_CHUNK), lambda b, h, i: (b, h, i, 0)),
                   pl.BlockSpec((1, 1, tt // GDN_CHUNK, LANES), lambda b, h, i: (b, h, i, 0))],
        out_shape=[jax.ShapeDtypeStruct((t, HEADS_W), F32),
                   jax.ShapeDtypeStruct((t, HEADS_W), BF16),
                   jax.ShapeDtypeStruct((t, HEADS_W), BF16),
                   jax.ShapeDtypeStruct((t, HEADS_W), BF16),
                   jax.ShapeDtypeStruct((bsz, N_HEADS, seq, GDN_CHUNK), BF16),
                   jax.ShapeDtypeStruct((bsz, N_HEADS, seq // GDN_CHUNK, LANES), F32)],
        compiler_params=_params(("arbitrary", "arbitrary", "arbitrary")),
        name="gdn_intra",
    )(qn, kn, vn, gates_col, gates_row)


GDN_HEADS_PER_STEP = 4


def _gdn_scan_kernel(u_ref, w_ref, qd_ref, kd_ref, qk_ref, gl_ref, z_ref, og_ref, o_ref, state_ref, *, ts):
    s_idx = pl.program_id(2)

    @pl.when(s_idx == 0)
    def _():
        state_ref[...] = jnp.zeros_like(state_ref)

    tn = (((0,), (0,)), ((), ()))

    def body(c, carry):
        r0 = pl.multiple_of(c * GDN_CHUNK, GDN_CHUNK)
        for g in range(GDN_HEADS_PER_STEP):
            cols = slice(g * HEAD_DIM, (g + 1) * HEAD_DIM)
            st = state_ref[g]
            st16 = st.astype(BF16)
            v_new = u_ref[pl.ds(r0, GDN_CHUNK), cols] - jnp.dot(
                w_ref[pl.ds(r0, GDN_CHUNK), cols], st16, preferred_element_type=F32)
            vn16 = v_new.astype(BF16)
            o = jnp.dot(qd_ref[pl.ds(r0, GDN_CHUNK), cols], st16, preferred_element_type=F32)
            o = o + jnp.dot(qk_ref[0, g, pl.ds(r0, GDN_CHUNK), :], vn16, preferred_element_type=F32)
            decay = gl_ref[0, g, pl.ds(c, 1), :]
            state_ref[g] = st * decay + lax.dot_general(kd_ref[pl.ds(r0, GDN_CHUNK), cols], vn16, tn,
                                                        preferred_element_type=F32)
            oms = jnp.mean(o * o, axis=-1, keepdims=True)
            z = z_ref[pl.ds(r0, GDN_CHUNK), cols].astype(F32)
            o_ref[pl.ds(r0, GDN_CHUNK), cols] = (o * lax.rsqrt(oms + EPS) * og_ref[...] * _silu(z)).astype(o_ref.dtype)
        return carry

    lax.fori_loop(0, ts // GDN_CHUNK, body, 0)


def gdn_scan(u, w, qd, kd, qk, gl, proj, on_g, bsz, seq, ts=1024):
    t = u.shape[0]
    gw = GDN_HEADS_PER_STEP * HEAD_DIM
    ts = min(ts, seq)
    per_b = seq // ts
    n_hg = N_HEADS // GDN_HEADS_PER_STEP
    hb = pl.BlockSpec((ts, gw), lambda b, hg, s: (b * per_b + s, hg))
    return pl.pallas_call(
        functools.partial(_gdn_scan_kernel, ts=ts),
        grid=(bsz, n_hg, per_b),
        in_specs=[hb, hb, hb, hb,
                  pl.BlockSpec((1, GDN_HEADS_PER_STEP, ts, GDN_CHUNK), lambda b, hg, s: (b, hg, s, 0)),
                  pl.BlockSpec((1, GDN_HEADS_PER_STEP, ts // GDN_CHUNK, LANES), lambda b, hg, s: (b, hg, s, 0)),
                  pl.BlockSpec((ts, gw), lambda b, hg, s: (b * per_b + s, COL_DZ // gw + hg)),
                  pl.BlockSpec((1, HEAD_DIM), lambda b, hg, s: (0, 0))],
        out_specs=hb,
        out_shape=jax.ShapeDtypeStruct((t, HEADS_W), BF16),
        scratch_shapes=[pltpu.VMEM((GDN_HEADS_PER_STEP, HEAD_DIM, HEAD_DIM), F32)],
        compiler_params=_params(("arbitrary", "arbitrary", "arbitrary")),
        name="gdn_scan",
    )(u, w, qd, kd, qk, gl, proj, on_g.reshape(1, -1))


def _first_top2(vals):
    n_v = len(vals)
    m1 = vals[0]
    for v in vals[1:]:
        m1 = jnp.maximum(m1, v)
    i1 = jnp.full(m1.shape, n_v - 1, jnp.int32)
    for k in range(n_v - 2, -1, -1):
        i1 = jnp.where(vals[k] == m1, k, i1)
    rest = [jnp.where(i1 == k, -jnp.inf, vals[k]) for k in range(n_v)]
    m2 = rest[0]
    for v in rest[1:]:
        m2 = jnp.maximum(m2, v)
    i2 = jnp.full(m1.shape, n_v - 1, jnp.int32)
    for k in range(n_v - 2, -1, -1):
        i2 = jnp.where(jnp.logical_and(rest[k] == m2, i1 != k), k, i2)
    return m1, i1, m2, i2


def _routing_kernel(lg_ref, rb_ref, eidx_ref, gcol_ref, scr_ref):
    lt = lg_ref[...].T
    scores = [_sigmoid(lt[e:e + 1, :]) for e in range(N_EXPERTS)]
    sel = [scores[e] + rb_ref[e:e + 1, :] for e in range(N_EXPERTS)]
    g_score, g_i1, g_i2 = [], [], []
    for g in range(N_EXPERT_GROUPS):
        m1, i1, m2, i2 = _first_top2(sel[g * EXPERTS_PER_GROUP:(g + 1) * EXPERTS_PER_GROUP])
        g_score.append(m1 + m2)
        g_i1.append(i1)
        g_i2.append(i2)
    _, best, _, _ = _first_top2(g_score)
    l1 = g_i1[N_EXPERT_GROUPS - 1]
    l2 = g_i2[N_EXPERT_GROUPS - 1]
    for g in range(N_EXPERT_GROUPS - 2, -1, -1):
        l1 = jnp.where(best == g, g_i1[g], l1)
        l2 = jnp.where(best == g, g_i2[g], l2)
    e1 = best * EXPERTS_PER_GROUP + l1
    e2 = best * EXPERTS_PER_GROUP + l2
    s1 = jnp.zeros_like(scores[0])
    s2 = jnp.zeros_like(scores[0])
    for e in range(N_EXPERTS):
        s1 = jnp.where(e1 == e, scores[e], s1)
        s2 = jnp.where(e2 == e, scores[e], s2)
    tot = s1 + s2
    eidx_ref[...] = jnp.zeros_like(eidx_ref)
    eidx_ref[0:1, :] = e1
    eidx_ref[1:2, :] = e2
    scr_ref[...] = jnp.zeros_like(scr_ref)
    scr_ref[0:1, :] = s1 / tot
    scr_ref[1:2, :] = s2 / tot
    gcol_ref[...] = scr_ref[...].T


def moe_routing(logits, router_bias, tm=1024):
    t = logits.shape[0]
    tm = min(tm, t)
    rb = jnp.broadcast_to(jnp.zeros((LANES,), F32).at[:N_EXPERTS].set(router_bias)[:, None], (LANES, tm))
    return pl.pallas_call(
        _routing_kernel,
        grid=(t // tm,),
        in_specs=[pl.BlockSpec((tm, LANES), lambda i: (i, 0)), pl.BlockSpec((LANES, tm), lambda i: (0, 0))],
        out_specs=[pl.BlockSpec((8, tm), lambda i: (0, i)), pl.BlockSpec((tm, LANES), lambda i: (i, 0))],
        out_shape=[jax.ShapeDtypeStruct((8, t), jnp.int32), jax.ShapeDtypeStruct((t, LANES), F32)],
        scratch_shapes=[pltpu.VMEM((LANES, tm), F32)],
        compiler_params=_params(("arbitrary",)),
        name="moe_routing",
    )(logits, rb)


MOE_BLOCK = 256
POS_CHUNK = 256


def _positions_kernel(eidx_ref, dest_ref, meta_ref, *, t):
    e_full = lax.broadcasted_iota(jnp.int32, (N_EXPERTS, t), 0)
    counts = jnp.zeros((N_EXPERTS, 1), F32)
    for k in range(TOP_K):
        oh = jnp.where(e_full == eidx_ref[k:k + 1, :], 1.0, 0.0)
        counts = counts + jnp.sum(oh, axis=1, keepdims=True)
    padded = jnp.floor((counts + (MOE_BLOCK - 1)) * (1.0 / MOE_BLOCK)) * MOE_BLOCK
    e_col = lax.broadcasted_iota(jnp.int32, (N_EXPERTS, 1), 0)
    pad_start = jnp.zeros((N_EXPERTS, 1), F32)
    running = jnp.zeros((1, 1), F32)
    for e in range(N_EXPERTS):
        pad_start = jnp.where(e_col == e, running, pad_start)
        running = running + padded[e:e + 1, :]
    pad_end = pad_start + padded

    r = lax.broadcasted_iota(jnp.int32, (POS_CHUNK, POS_CHUNK), 0)
    c = lax.broadcasted_iota(jnp.int32, (POS_CHUNK, POS_CHUNK), 1)
    upper = jnp.where(r <= c, 1.0, 0.0).astype(BF16)
    e_blk_iota = lax.broadcasted_iota(jnp.int32, (N_EXPERTS, POS_CHUNK), 0)
    dest_ref[...] = jnp.zeros_like(dest_ref)
    carry = pad_start - 1.0
    for k in range(TOP_K):
        def body(j, carry, k=k):
            c0 = pl.multiple_of(j * POS_CHUNK, POS_CHUNK)
            hit = e_blk_iota == eidx_ref[k:k + 1, pl.ds(c0, POS_CHUNK)]
            pref = jnp.dot(jnp.where(hit, 1.0, 0.0).astype(BF16), upper, preferred_element_type=F32)
            pos = jnp.sum(jnp.where(hit, carry + pref, 0.0), axis=0, keepdims=True)
            dest_ref[k:k + 1, pl.ds(c0, POS_CHUNK)] = pos.astype(jnp.int32)
            return carry + pref[:, POS_CHUNK - 1:POS_CHUNK]
        carry = lax.fori_loop(0, t // POS_CHUNK, body, carry)

    blk_start = lax.broadcasted_iota(jnp.int32, (N_EXPERTS, LANES), 1).astype(F32) * MOE_BLOCK
    be = jnp.sum(jnp.where(pad_end <= blk_start, 1.0, 0.0), axis=0, keepdims=True)
    be = jnp.minimum(be, N_EXPERTS - 1.0)
    n_used = jnp.broadcast_to(running * (1.0 / MOE_BLOCK), (1, LANES))
    meta_ref[...] = jnp.zeros_like(meta_ref)
    meta_ref[0:1, :] = be.astype(jnp.int32)
    meta_ref[1:2, :] = n_used.astype(jnp.int32)


def moe_positions(eidx):
    t = eidx.shape[1]
    return pl.pallas_call(
        functools.partial(_positions_kernel, t=t),
        out_shape=[jax.ShapeDtypeStruct((8, t), jnp.int32), jax.ShapeDtypeStruct((8, LANES), jnp.int32)],
        compiler_params=pltpu.CompilerParams(vmem_limit_bytes=VMEM_LIMIT),
        name="moe_positions",
    )(eidx)


def _row_token_kernel(dest_ref, rt_ref, *, t, n_rows):
    def zero(r, carry):
        rt_ref[r] = 0
        return carry
    lax.fori_loop(0, n_rows, zero, 0)
    for k in range(TOP_K):
        def body(n, carry, k=k):
            rt_ref[dest_ref[k * t + n]] = n
            return carry
        lax.fori_loop(0, t, body, 0)


def moe_row_token(dest_flat, t, n_rows):
    return pl.pallas_call(
        functools.partial(_row_token_kernel, t=t, n_rows=n_rows),
        in_specs=[pl.BlockSpec(memory_space=pltpu.SMEM)],
        out_specs=pl.BlockSpec(memory_space=pltpu.SMEM),
        out_shape=jax.ShapeDtypeStruct((n_rows,), jnp.int32),
        name="moe_row_token",
    )(dest_flat)


def _gather_rows_kernel(rt_ref, meta_ref, h_ref, o_ref, buf_ref, sem):
    blk = pl.program_id(0)

    @pl.when(blk < meta_ref[LANES])
    def _():
        base = blk * MOE_BLOCK

        def row_copy(r):
            tok = rt_ref[base + r]
            return pltpu.make_async_copy(h_ref.at[pl.ds(tok, 1), :], buf_ref.at[pl.ds(r, 1), :], sem)

        def start(r, carry):
            row_copy(r).start()
            return carry
        lax.fori_loop(0, MOE_BLOCK, start, 0)

        def wait(r, carry):
            row_copy(r).wait()
            return carry
        lax.fori_loop(0, MOE_BLOCK, wait, 0)
        o_ref[...] = buf_ref[...].astype(o_ref.dtype)

    @pl.when(blk >= meta_ref[LANES])
    def _():
        o_ref[...] = jnp.zeros_like(o_ref)


def moe_gather_rows(row_token, meta_flat, h, n_rows):
    d = h.shape[1]
    n_blocks = n_rows // MOE_BLOCK
    grid_spec = pltpu.PrefetchScalarGridSpec(
        num_scalar_prefetch=2,
        grid=(n_blocks,),
        in_specs=[pl.BlockSpec(memory_space=pl.ANY)],
        out_specs=pl.BlockSpec((MOE_BLOCK, d), lambda i, rt, meta: (i, 0)),
        scratch_shapes=[pltpu.VMEM((MOE_BLOCK, d), F32), pltpu.SemaphoreType.DMA],
    )
    return pl.pallas_call(
        _gather_rows_kernel,
        grid_spec=grid_spec,
        out_shape=jax.ShapeDtypeStruct((n_rows, d), BF16),
        compiler_params=_params(("arbitrary",)),
        name="moe_gather_rows",
    )(row_token, meta_flat, h)


def _expert_changed(meta_ref, blk):
    prev = meta_ref[jnp.maximum(blk - 1, 0)]
    return jnp.logical_or(blk == 0, meta_ref[blk] != prev)


def _moe_up_kernel(meta_ref, x_ref, wg_ref, wu_ref, o_ref, wg16_ref, wu16_ref):
    blk = pl.program_id(1)

    @pl.when(blk < meta_ref[LANES])
    def _():
        @pl.when(_expert_changed(meta_ref, blk))
        def _():
            wg16_ref[...] = wg_ref[0].astype(BF16)
            wu16_ref[...] = wu_ref[0].astype(BF16)
        x = x_ref[...]
        a = jnp.dot(x, wg16_ref[...], preferred_element_type=F32)
        b = jnp.dot(x, wu16_ref[...], preferred_element_type=F32)
        o_ref[...] = (_silu(a) * b).astype(o_ref.dtype)

    @pl.when(blk >= meta_ref[LANES])
    def _():
        o_ref[...] = jnp.zeros_like(o_ref)


def _moe_down_kernel(meta_ref, h_ref, wd_ref, o_ref, wd16_ref):
    blk = pl.program_id(1)

    @pl.when(blk < meta_ref[LANES])
    def _():
        @pl.when(_expert_changed(meta_ref, blk))
        def _():
            wd16_ref[...] = wd_ref[0].astype(BF16)
        o_ref[...] = jnp.dot(h_ref[...], wd16_ref[...], preferred_element_type=F32).astype(o_ref.dtype)

    @pl.when(blk >= meta_ref[LANES])
    def _():
        o_ref[...] = jnp.zeros_like(o_ref)


def _used(i, meta):
    return jnp.minimum(i, meta[LANES] - 1)


def moe_up(meta_flat, xs, w_gate, w_up, tn=512):
    n_rows, d = xs.shape
    de = w_gate.shape[2]
    n_blocks = n_rows // MOE_BLOCK
    wspec = pl.BlockSpec((1, d, tn), lambda c, i, meta: (meta[_used(i, meta)], 0, c))
    grid_spec = pltpu.PrefetchScalarGridSpec(
        num_scalar_prefetch=1,
        grid=(de // tn, n_blocks),
        in_specs=[pl.BlockSpec((MOE_BLOCK, d), lambda c, i, meta: (_used(i, meta), 0)), wspec, wspec],
        out_specs=pl.BlockSpec((MOE_BLOCK, tn), lambda c, i, meta: (i, c)),
        scratch_shapes=[pltpu.VMEM((d, tn), BF16), pltpu.VMEM((d, tn), BF16)],
    )
    return pl.pallas_call(
        _moe_up_kernel,
        grid_spec=grid_spec,
        out_shape=jax.ShapeDtypeStruct((n_rows, de), BF16),
        compiler_params=_params(("arbitrary", "arbitrary")),
        name="moe_up",
    )(meta_flat, xs, w_gate, w_up)


def moe_down(meta_flat, hs, w_down, tn=2048):
    n_rows, de = hs.shape
    d = w_down.shape[2]
    n_blocks = n_rows // MOE_BLOCK
    grid_spec = pltpu.PrefetchScalarGridSpec(
        num_scalar_prefetch=1,
        grid=(d // tn, n_blocks),
        in_specs=[pl.BlockSpec((MOE_BLOCK, de), lambda c, i, meta: (_used(i, meta), 0)),
                  pl.BlockSpec((1, de, tn), lambda c, i, meta: (meta[_used(i, meta)], 0, c))],
        out_specs=pl.BlockSpec((MOE_BLOCK, tn), lambda c, i, meta: (i, c)),
        scratch_shapes=[pltpu.VMEM((de, tn), BF16)],
    )
    return pl.pallas_call(
        _moe_down_kernel,
        grid_spec=grid_spec,
        out_shape=jax.ShapeDtypeStruct((n_rows, d), F32),
        compiler_params=_params(("arbitrary", "arbitrary")),
        name="moe_down",
    )(meta_flat, hs, w_down)


def _combine_kernel(dest_ref, rows_ref, x_ref, gcol_ref, gate_ref, fg_ref, o_ref, buf_ref, sem, *, tm, t, final):
    i = pl.program_id(0)
    base = i * tm

    def row_copy(k, r):
        src = dest_ref[k * t + base + r]
        return pltpu.make_async_copy(rows_ref.at[pl.ds(src, 1), :], buf_ref.at[k, pl.ds(r, 1), :], sem)

    def start(r, carry):
        for k in range(TOP_K):
            row_copy(k, r).start()
        return carry
    lax.fori_loop(0, tm, start, 0)

    def wait(r, carry):
        for k in range(TOP_K):
            row_copy(k, r).wait()
        return carry
    lax.fori_loop(0, tm, wait, 0)

    g = gcol_ref[...]
    y = _lane_col(g, 0) * buf_ref[0] + _lane_col(g, 1) * buf_ref[1]
    xn = x_ref[...] + gate_ref[0] * y
    if final:
        ms = jnp.mean(xn * xn, axis=-1, keepdims=True)
        xn = xn * lax.rsqrt(ms + EPS) * fg_ref[...]
    o_ref[...] = xn


def moe_combine(dest_flat, out_rows, x2, gates_col, gate_f, final_g, seq, final, tm=256):
    t, d = x2.shape
    bsz = gate_f.shape[0]
    per_b = seq // tm
    grid_spec = pltpu.PrefetchScalarGridSpec(
        num_scalar_prefetch=1,
        grid=(t // tm,),
        in_specs=[pl.BlockSpec(memory_space=pl.ANY),
                  pl.BlockSpec((tm, d), lambda i, dest: (i, 0)),
                  pl.BlockSpec((tm, LANES), lambda i, dest: (i, 0)),
                  pl.BlockSpec((1, 1, d), lambda i, dest: (i // per_b, 0, 0)),
                  pl.BlockSpec((1, d), lambda i, dest: (0, 0))],
        out_specs=pl.BlockSpec((tm, d), lambda i, dest: (i, 0)),
        scratch_shapes=[pltpu.VMEM((TOP_K, tm, d), F32), pltpu.SemaphoreType.DMA],
    )
    return pl.pallas_call(
        functools.partial(_combine_kernel, tm=tm, t=t, final=final),
        grid_spec=grid_spec,
        out_shape=jax.ShapeDtypeStruct((t, d), F32),
        compiler_params=_params(("arbitrary",)),
        name="moe_combine",
    )(dest_flat, out_rows, x2, gates_col, gate_f.reshape(bsz, 1, d), final_g.reshape(1, d))


def moe_ffn(h32, logits, router_bias, w_gate, w_up, w_down, x2, gate_f, final_g, seq, final):
    t = h32.shape[0]
    n_rows = t * TOP_K + N_EXPERTS * MOE_BLOCK
    eidx, gates_col = moe_routing(logits, router_bias)
    dest, meta = moe_positions(eidx)
    dest_flat = dest[:TOP_K].reshape(TOP_K * t)
    meta_flat = meta[:2].reshape(2 * LANES)
    row_token = moe_row_token(dest_flat, t, n_rows)
    xs = moe_gather_rows(row_token, meta_flat, h32, n_rows)
    hs = moe_up(meta_flat, xs, w_gate, w_up)
    out_rows = moe_down(meta_flat, hs, w_down)
    return moe_combine(dest_flat, out_rows, x2, gates_col, gate_f, final_g, seq, final)


def _prep_w_in(w_in_l):
    cuts = {}
    off = 0
    for name, width in (("aval", CONV_CH), ("agate", CONV_CH), ("fq", HEADS_W), ("fk", HEADS_W),
                        ("fv", HEADS_W), ("ff", N_HEADS), ("fg", HEADS_W), ("dq", HEADS_W),
                        ("dk", HEADS_W), ("dv", HEADS_W), ("db", N_HEADS), ("da", N_HEADS), ("dz", HEADS_W)):
        cuts[name] = w_in_l[:, off:off + width]
        off += width
    big = jnp.concatenate([cuts[n] for n in ("fq", "fk", "fv", "fg", "dq", "dk", "dv", "dz", "aval", "agate")],
                          axis=1).astype(BF16)
    d = w_in_l.shape[0]
    small = jnp.concatenate([cuts["ff"], cuts["db"], cuts["da"],
                             jnp.zeros((d, LANES - 3 * N_HEADS), w_in_l.dtype)], axis=1).astype(BF16)
    return big, small


def kernel(x, c, ada_w, ada_b, norm_mix_g, norm_ffn_g, w_in, conv_w, conv_b, conv_ln_g, conv_ln_b,
           fox_f_bias, fox_qn_g, fox_kn_g, fox_on_g, dn_conv_w, dn_a_log, dn_dt_bias, dn_on_g, w_out,
           w_router, router_bias, w_gate_e, w_up_e, w_down_e, final_g):
    bsz, seq, d = x.shape
    depth = ada_w.shape[0]
    t = bsz * seq
    x2 = x.reshape(t, d)
    mod = adaln(c, ada_w, ada_b)
    for l in range(depth):
        shift_m, scale_m, gate_m, shift_f, scale_f, gate_f = [mod[l, :, k * d:(k + 1) * d] for k in range(6)]
        w_big, w_small = _prep_w_in(w_in[l])
        h = norm_mod(x2, norm_mix_g[l], scale_m, shift_m, seq)
        proj = matmul(h, w_big, BF16)
        small = matmul(h, w_small, F32)
        gates_col, gates_row = gates_prep(small, fox_f_bias[l], dn_a_log[l], dn_dt_bias[l], bsz, seq)
        y_a = conv_module(proj, conv_w[l], conv_b[l], conv_ln_g[l], conv_ln_b[l], bsz, seq)
        y_b = fox_attention(proj, gates_col, gates_row, fox_qn_g[l], fox_kn_g[l], fox_on_g[l], bsz, seq)
        qn, kn, vn = gdn_prep(proj, dn_conv_w[l], bsz, seq)
        u, w, qd, kd, qk, gl = gdn_intra(qn, kn, vn, gates_col, gates_row, bsz, seq)
        y_c = gdn_scan(u, w, qd, kd, qk, gl, proj, dn_on_g[l], bsz, seq)
        x2 = out_proj(y_a, y_b, y_c, w_out[l].astype(BF16), x2, gate_m, seq)
        h32, logits = norm_mod(x2, norm_ffn_g[l], scale_f, shift_f, seq, w_router=w_router)
        x2 = moe_ffn(h32, logits, router_bias, w_gate_e[l], w_up_e[l], w_down_e[l], x2, gate_f, final_g,
                     seq, final=(l == depth - 1))
    return x2.reshape(bsz, seq, d)
```

```python
import functools

import jax
import jax.numpy as jnp
from jax import lax
from jax.experimental import pallas as pl
from jax.experimental.pallas import tpu as pltpu

F32 = jnp.float32
BF16 = jnp.bfloat16

HEAD_DIM = 128
N_HEADS = 12
HEADS_W = N_HEADS * HEAD_DIM
CONV_CH = 1024
CONF_KERNEL = 31
SHORT_CONV = 4
GDN_CHUNK = 64
N_EXPERTS = 16
N_EXPERT_GROUPS = 4
EXPERTS_PER_GROUP = 4
TOP_K = 2
D_EXPERT = 1024
EPS = 1e-6
LANES = 128
VMEM_LIMIT = 56 * 1024 * 1024

COL_AVAL = 0
COL_AGATE = COL_AVAL + CONV_CH
COL_FQ = COL_AGATE + CONV_CH
COL_FK = COL_FQ + HEADS_W
COL_FV = COL_FK + HEADS_W
COL_FG = COL_FV + HEADS_W
COL_DQ = COL_FG + HEADS_W
COL_DK = COL_DQ + HEADS_W
COL_DV = COL_DK + HEADS_W
COL_DZ = COL_DV + HEADS_W
N_BIG = COL_DZ + HEADS_W
IN_TN = 512
IN_REGIONS = ((0, COL_FG // IN_TN, 0),
              (COL_FG // IN_TN, COL_DZ // IN_TN, N_HEADS),
              (COL_DZ // IN_TN, N_BIG // IN_TN, 3 * N_HEADS))
SRC_FF_BLOCK = COL_FG // LANES
SRC_DB_BLOCK = COL_DZ // LANES
LANE_FF = 0
LANE_DB = N_HEADS
LANE_DA = 2 * N_HEADS


def _params(sem, vmem=VMEM_LIMIT):
    return pltpu.CompilerParams(dimension_semantics=sem, vmem_limit_bytes=vmem)


def _sigmoid(x):
    return 1.0 / (1.0 + jnp.exp(-x))


def _silu(x):
    return x * _sigmoid(x)


def _adaln_kernel(c_ref, w_ref, b_ref, o_ref):
    c = c_ref[...]
    ca = _silu(c).astype(BF16)
    o_ref[0] = jnp.dot(ca, w_ref[0].astype(BF16), preferred_element_type=F32) + b_ref[0]


def adaln(c, ada_w, ada_b, tn=1024):
    depth, d, n = ada_w.shape
    bsz = c.shape[0]
    cp = jnp.zeros((8, d), F32).at[:bsz].set(c)
    out = pl.pallas_call(
        _adaln_kernel,
        grid=(depth, n // tn),
        in_specs=[
            pl.BlockSpec((8, d), lambda l, j: (0, 0)),
            pl.BlockSpec((1, d, tn), lambda l, j: (l, 0, j)),
            pl.BlockSpec((1, 1, tn), lambda l, j: (l, 0, j)),
        ],
        out_specs=pl.BlockSpec((1, 8, tn), lambda l, j: (l, 0, j)),
        out_shape=jax.ShapeDtypeStruct((depth, 8, n), F32),
        compiler_params=_params(("arbitrary", "arbitrary")),
        name="adaln",
    )(cp, ada_w, ada_b.reshape(depth, 1, n))
    return out[:, :bsz]


def _norm_mod_kernel(x_ref, g_ref, sc_ref, sh_ref, o_ref):
    x = x_ref[...]
    ms = jnp.mean(x * x, axis=-1, keepdims=True)
    y = x * lax.rsqrt(ms + EPS) * g_ref[...]
    o_ref[...] = (y * (1.0 + sc_ref[0]) + sh_ref[0]).astype(o_ref.dtype)


def _norm_mod_router_kernel(x_ref, g_ref, sc_ref, sh_ref, wr_ref, o_ref, r_ref):
    x = x_ref[...]
    ms = jnp.mean(x * x, axis=-1, keepdims=True)
    y = x * lax.rsqrt(ms + EPS) * g_ref[...]
    h = y * (1.0 + sc_ref[0]) + sh_ref[0]
    o_ref[...] = h.astype(o_ref.dtype)
    r_ref[...] = jnp.dot(h, wr_ref[...], preferred_element_type=F32, precision=lax.Precision.HIGHEST)


def norm_mod(x2, g, scale, shift, seq, w_router=None, tm=256):
    t, d = x2.shape
    bsz = scale.shape[0]
    per_b = seq // tm
    in_specs = [
        pl.BlockSpec((tm, d), lambda i: (i, 0)),
        pl.BlockSpec((1, d), lambda i: (0, 0)),
        pl.BlockSpec((1, 1, d), lambda i: (i // per_b, 0, 0)),
        pl.BlockSpec((1, 1, d), lambda i: (i // per_b, 0, 0)),
    ]
    args = [x2, g.reshape(1, d), scale.reshape(bsz, 1, d), shift.reshape(bsz, 1, d)]
    if w_router is None:
        return pl.pallas_call(
            _norm_mod_kernel,
            grid=(t // tm,),
            in_specs=in_specs,
            out_specs=pl.BlockSpec((tm, d), lambda i: (i, 0)),
            out_shape=jax.ShapeDtypeStruct((t, d), BF16),
            compiler_params=_params(("arbitrary",)),
            name="norm_mod",
        )(*args)
    n_e = w_router.shape[1]
    wr = jnp.zeros((d, LANES), F32).at[:, :n_e].set(w_router)
    return pl.pallas_call(
        _norm_mod_router_kernel,
        grid=(t // tm,),
        in_specs=in_specs + [pl.BlockSpec((d, LANES), lambda i: (0, 0))],
        out_specs=[pl.BlockSpec((tm, d), lambda i: (i, 0)), pl.BlockSpec((tm, LANES), lambda i: (i, 0))],
        out_shape=[jax.ShapeDtypeStruct((t, d), F32), jax.ShapeDtypeStruct((t, LANES), F32)],
        compiler_params=_params(("arbitrary",)),
        name="norm_mod_router",
    )(*args, wr)


W_CAST_ROWS = 512


def _in_proj_kernel(h_ref, wa_ref, wb_ref, o_ref, w16_ref):
    j = pl.program_id(0)
    i = pl.program_id(1)
    k = w16_ref.shape[0]

    for j0, j1, shift in IN_REGIONS:
        @pl.when(jnp.logical_and(i == 0, jnp.logical_and(j >= j0, j < j1)))
        def _(shift=shift):
            def body(r, carry):
                r0 = pl.multiple_of(r * W_CAST_ROWS, W_CAST_ROWS)
                a = wa_ref[0, pl.ds(r0, W_CAST_ROWS), :]
                if shift:
                    cat = jnp.concatenate([a, wb_ref[0, pl.ds(r0, W_CAST_ROWS), :]], axis=1)
                    a = pltpu.roll(cat, IN_TN + LANES - shift, axis=1)[:, :IN_TN]
                w16_ref[pl.ds(r0, W_CAST_ROWS), :] = a.astype(BF16)
                return carry
            lax.fori_loop(0, k // W_CAST_ROWS, body, 0)

    o_ref[...] = jnp.dot(h_ref[...], w16_ref[...], preferred_element_type=F32).astype(o_ref.dtype)


def in_proj(h, w_in, layer, tm=1024):
    m, k = h.shape
    tm = min(tm, m)
    return pl.pallas_call(
        _in_proj_kernel,
        grid=(N_BIG // IN_TN, m // tm),
        in_specs=[pl.BlockSpec((tm, k), lambda j, i: (i, 0)),
                  pl.BlockSpec((1, k, IN_TN), lambda j, i: (layer, 0, j)),
                  pl.BlockSpec((1, k, LANES), lambda j, i: (layer, 0, (j + 1) * (IN_TN // LANES)))],
        out_specs=pl.BlockSpec((tm, IN_TN), lambda j, i: (i, j)),
        out_shape=jax.ShapeDtypeStruct((m, N_BIG), BF16),
        scratch_shapes=[pltpu.VMEM((k, IN_TN), BF16)],
        compiler_params=_params(("arbitrary", "arbitrary")),
        name="in_proj",
    )(h, w_in, w_in)


def _cast_weight_tile(w_ref, w16_ref):
    def body(r, carry):
        r0 = pl.multiple_of(r * W_CAST_ROWS, W_CAST_ROWS)
        w16_ref[pl.ds(r0, W_CAST_ROWS), :] = w_ref[pl.ds(r0, W_CAST_ROWS), :].astype(BF16)
        return carry
    lax.fori_loop(0, w16_ref.shape[0] // W_CAST_ROWS, body, 0)


def _out_proj_kernel(ya_ref, yb_ref, yc_ref, w_ref, x_ref, gate_ref, o_ref, w16_ref, y_ref):
    @pl.when(pl.program_id(1) == 0)
    def _():
        _cast_weight_tile(w_ref.at[0], w16_ref)

    ka, kb = ya_ref.shape[1], yb_ref.shape[1]
    y_ref[:, 0:ka] = ya_ref[...]
    y_ref[:, ka:ka + kb] = yb_ref[...]
    y_ref[:, ka + kb:] = yc_ref[...]
    acc = jnp.dot(y_ref[...], w16_ref[...], preferred_element_type=F32)
    o_ref[...] = x_ref[...] + gate_ref[0] * acc


def out_proj(ya, yb, yc, w_out, layer, x2, gate, seq, tm=512, tn=512):
    t, d = x2.shape
    bsz = gate.shape[0]
    per_b = seq // tm
    ka, kb, kc = ya.shape[1], yb.shape[1], yc.shape[1]
    k = ka + kb + kc
    return pl.pallas_call(
        _out_proj_kernel,
        grid=(d // tn, t // tm),
        in_specs=[
            pl.BlockSpec((tm, ka), lambda j, i: (i, 0)),
            pl.BlockSpec((tm, kb), lambda j, i: (i, 0)),
            pl.BlockSpec((tm, kc), lambda j, i: (i, 0)),
            pl.BlockSpec((1, k, tn), lambda j, i: (layer, 0, j)),
            pl.BlockSpec((tm, tn), lambda j, i: (i, j)),
            pl.BlockSpec((1, 1, tn), lambda j, i: (i // per_b, 0, j)),
        ],
        out_specs=pl.BlockSpec((tm, tn), lambda j, i: (i, j)),
        out_shape=jax.ShapeDtypeStruct((t, d), F32),
        scratch_shapes=[pltpu.VMEM((k, tn), BF16), pltpu.VMEM((tm, k), BF16)],
        compiler_params=_params(("arbitrary", "arbitrary")),
        name="out_proj",
    )(ya, yb, yc, w_out, x2, gate.reshape(bsz, 1, d))


CONV_HALO = 32


def _conv_module_kernel(val_ref, gate_ref, w_ref, b_ref, g_ref, beta_ref, o_ref, buf_ref, acc_ref, *, tt):
    i = pl.program_id(1)

    @pl.when(i == 0)
    def _():
        buf_ref[0:CONV_HALO, :] = jnp.zeros((CONV_HALO, CONV_CH), F32)

    a = val_ref[...].astype(F32) * _sigmoid(gate_ref[...].astype(F32))
    buf_ref[CONV_HALO:CONV_HALO + tt, :] = a

    rc = 64
    off = CONV_HALO - (CONF_KERNEL - 1)

    def col_body(c, carry):
        c0 = pl.multiple_of(c * LANES, LANES)
        w = w_ref[:, pl.ds(c0, LANES)]
        bias = b_ref[:, pl.ds(c0, LANES)]
        for r in range(tt // rc):
            acc = jnp.broadcast_to(bias, (rc, LANES))
            for k in range(CONF_KERNEL):
                acc = acc + w[k:k + 1, :] * buf_ref[r * rc + off + k:r * rc + off + k + rc, pl.ds(c0, LANES)]
            acc_ref[r * rc:(r + 1) * rc, pl.ds(c0, LANES)] = acc
        return carry

    lax.fori_loop(0, CONV_CH // LANES, col_body, 0)

    buf_ref[0:CONV_HALO, :] = buf_ref[tt:tt + CONV_HALO, :]

    y = acc_ref[...]
    mu = jnp.mean(y, axis=-1, keepdims=True)
    yc = y - mu
    var = jnp.mean(yc * yc, axis=-1, keepdims=True)
    z = yc * lax.rsqrt(var + EPS) * g_ref[...] + beta_ref[...]
    o_ref[...] = _silu(z).astype(o_ref.dtype)


def conv_module(proj, conv_w, conv_b, ln_g, ln_b, bsz, seq, tt=256):
    t = proj.shape[0]
    per_b = seq // tt
    return pl.pallas_call(
        functools.partial(_conv_module_kernel, tt=tt),
        grid=(bsz, per_b),
        in_specs=[
            pl.BlockSpec((tt, CONV_CH), lambda b, i: (b * per_b + i, COL_AVAL // CONV_CH)),
            pl.BlockSpec((tt, CONV_CH), lambda b, i: (b * per_b + i, COL_AGATE // CONV_CH)),
            pl.BlockSpec((CONF_KERNEL, CONV_CH), lambda b, i: (0, 0)),
            pl.BlockSpec((1, CONV_CH), lambda b, i: (0, 0)),
            pl.BlockSpec((1, CONV_CH), lambda b, i: (0, 0)),
            pl.BlockSpec((1, CONV_CH), lambda b, i: (0, 0)),
        ],
        out_specs=pl.BlockSpec((tt, CONV_CH), lambda b, i: (b * per_b + i, 0)),
        out_shape=jax.ShapeDtypeStruct((t, CONV_CH), BF16),
        scratch_shapes=[pltpu.VMEM((tt + CONV_HALO, CONV_CH), F32), pltpu.VMEM((tt, CONV_CH), F32)],
        compiler_params=_params(("arbitrary", "arbitrary")),
        name="conv_module",
    )(proj, proj, conv_w, conv_b.reshape(1, -1), ln_g.reshape(1, -1), ln_b.reshape(1, -1))


def _softplus(x):
    return jnp.maximum(x, 0.0) + jnp.log(1.0 + jnp.exp(-jnp.abs(x)))


def _gates_kernel(h_ref, wf_ref, wd_ref, fb_ref, al_ref, dtb_ref, col_ref, row_ref, carry_ref, w16_ref, *, tt):
    i = pl.program_id(1)

    @pl.when(jnp.logical_and(pl.program_id(0) == 0, i == 0))
    def _():
        wl = lax.broadcasted_iota(jnp.int32, wf_ref.shape[1:], 1)
        w = jnp.where(wl < LANE_DB, wf_ref[0], jnp.where(wl < LANE_DA + N_HEADS, wd_ref[0], 0.0))
        w16_ref[...] = w.astype(BF16)

    @pl.when(i == 0)
    def _():
        carry_ref[...] = jnp.zeros_like(carry_ref)

    s = jnp.dot(h_ref[...], w16_ref[...], preferred_element_type=F32)
    lane = lax.broadcasted_iota(jnp.int32, (tt, LANES), 1)
    is_f = lane < LANE_DB
    is_b = jnp.logical_and(lane >= LANE_DB, lane < LANE_DA)
    is_g = jnp.logical_and(lane >= LANE_DA, lane < LANE_DA + N_HEADS)
    log_f = -_softplus(-(s + fb_ref[...]))
    beta = _sigmoid(s)
    g = -jnp.exp(al_ref[...]) * _softplus(s + dtb_ref[...])
    vals = jnp.where(is_f, log_f, jnp.where(is_g, g, 0.0))
    r = lax.broadcasted_iota(jnp.int32, (tt, tt), 0)
    c = lax.broadcasted_iota(jnp.int32, (tt, tt), 1)
    tri = r >= c
    l_full = jnp.where(tri, 1.0, 0.0).astype(F32)
    l_blk = jnp.where(jnp.logical_and(tri, (r // GDN_CHUNK) == (c // GDN_CHUNK)), 1.0, 0.0).astype(F32)
    hp = lax.Precision.HIGHEST
    full = jnp.dot(l_full, vals, preferred_element_type=F32, precision=hp) + carry_ref[0:1, :]
    blk = jnp.dot(l_blk, vals, preferred_element_type=F32, precision=hp)
    carry_ref[0:1, :] = full[tt - 1:tt, :]
    out = jnp.where(is_f, full, jnp.where(is_b, beta, jnp.where(is_g, blk, 0.0)))
    col_ref[...] = out
    row_ref[0] = out.T


def gates_prep(h, w_in, layer, f_bias, a_log, dt_bias, bsz, seq, tt=256):
    t, k = h.shape
    per_b = seq // tt
    fb = jnp.zeros((1, LANES), F32).at[0, LANE_FF:LANE_FF + N_HEADS].set(f_bias)
    al = jnp.zeros((1, LANES), F32).at[0, LANE_DA:LANE_DA + N_HEADS].set(a_log)
    dtb = jnp.zeros((1, LANES), F32).at[0, LANE_DA:LANE_DA + N_HEADS].set(dt_bias)
    vec = pl.BlockSpec((1, LANES), lambda b, i: (0, 0))
    return pl.pallas_call(
        functools.partial(_gates_kernel, tt=tt),
        grid=(bsz, per_b),
        in_specs=[pl.BlockSpec((tt, k), lambda b, i: (b * per_b + i, 0)),
                  pl.BlockSpec((1, k, LANES), lambda b, i: (layer, 0, SRC_FF_BLOCK)),
                  pl.BlockSpec((1, k, LANES), lambda b, i: (layer, 0, SRC_DB_BLOCK)),
                  vec, vec, vec],
        out_specs=[pl.BlockSpec((tt, LANES), lambda b, i: (b * per_b + i, 0)),
                   pl.BlockSpec((1, LANES, tt), lambda b, i: (b, 0, i))],
        out_shape=[jax.ShapeDtypeStruct((t, LANES), F32), jax.ShapeDtypeStruct((bsz, LANES, seq), F32)],
        scratch_shapes=[pltpu.VMEM((8, LANES), F32), pltpu.VMEM((k, LANES), BF16)],
        compiler_params=_params(("arbitrary", "arbitrary")),
        name="gates_prep",
    )(h, w_in, w_in, fb, al, dtb)


def _lane_col(x, idx):
    lane = lax.broadcasted_iota(jnp.int32, x.shape, 1)
    return jnp.sum(jnp.where(lane == idx, x, 0.0), axis=1, keepdims=True)


def _fox_kernel(q_ref, k_ref, v_ref, fg_ref, ccol_ref, crow_ref, qg_ref, kg_ref, og_ref, o_ref, kn_ref,
                *, tq, seq):
    h = pl.program_id(1)
    i = pl.program_id(2)
    tk = tq

    @pl.when(i == 0)
    def _():
        def kbody(c, carry):
            r0 = pl.multiple_of(c * tk, tk)
            kk = k_ref[pl.ds(r0, tk), :].astype(F32)
            ms = jnp.mean(kk * kk, axis=-1, keepdims=True)
            kn_ref[pl.ds(r0, tk), :] = (kk * lax.rsqrt(ms + EPS) * kg_ref[...]).astype(BF16)
            return carry
        lax.fori_loop(0, seq // tk, kbody, 0)

    q = q_ref[...].astype(F32)
    qms = jnp.mean(q * q, axis=-1, keepdims=True)
    q = (q * lax.rsqrt(qms + EPS) * qg_ref[...] * (HEAD_DIM ** -0.5)).astype(BF16)
    fq = _lane_col(ccol_ref[...], LANE_FF + h)

    def step(j, carry, masked):
        m, l, acc = carry
        r0 = pl.multiple_of(j * tk, tk)
        kc = kn_ref[pl.ds(r0, tk), :]
        s = lax.dot_general(q, kc, (((1,), (1,)), ((), ())), preferred_element_type=F32)
        fk = crow_ref[0, pl.ds(LANE_FF + h, 1), pl.ds(r0, tk)]
        s = s + fq - fk
        if masked:
            row = lax.broadcasted_iota(jnp.int32, (tq, tk), 0)
            col = lax.broadcasted_iota(jnp.int32, (tq, tk), 1)
            s = jnp.where(col <= row, s, -jnp.inf)
        m_new = jnp.maximum(m, jnp.max(s, axis=1, keepdims=True))
        alpha = jnp.exp(m - m_new)
        p = jnp.exp(s - m_new)
        l = alpha * l + jnp.sum(p, axis=1, keepdims=True)
        acc = alpha * acc + jnp.dot(p.astype(BF16), v_ref[pl.ds(r0, tk), :], preferred_element_type=F32)
        return m_new, l, acc

    init = (jnp.full((tq, 1), -jnp.inf, F32), jnp.zeros((tq, 1), F32), jnp.zeros((tq, HEAD_DIM), F32))
    carry = lax.fori_loop(0, i, lambda j, c: step(j, c, False), init)
    _, l, acc = step(i, carry, True)
    o = acc / l
    oms = jnp.mean(o * o, axis=-1, keepdims=True)
    o = o * lax.rsqrt(oms + EPS) * og_ref[...]
    o_ref[...] = (o * _sigmoid(fg_ref[...].astype(F32))).astype(o_ref.dtype)


def fox_attention(proj, gates_col, gates_row, qn_g, kn_g, on_g, bsz, seq, tq=512):
    t = proj.shape[0]
    nq = seq // tq
    cb = lambda col: col // HEAD_DIM
    vec = pl.BlockSpec((1, HEAD_DIM), lambda b, h, i: (0, 0))
    return pl.pallas_call(
        functools.partial(_fox_kernel, tq=tq, seq=seq),
        grid=(bsz, N_HEADS, nq),
        in_specs=[
            pl.BlockSpec((tq, HEAD_DIM), lambda b, h, i: (b * nq + i, cb(COL_FQ) + h)),
            pl.BlockSpec((seq, HEAD_DIM), lambda b, h, i: (b, cb(COL_FK) + h)),
            pl.BlockSpec((seq, HEAD_DIM), lambda b, h, i: (b, cb(COL_FV) + h)),
            pl.BlockSpec((tq, HEAD_DIM), lambda b, h, i: (b * nq + i, cb(COL_FG) + h)),
            pl.BlockSpec((tq, LANES), lambda b, h, i: (b * nq + i, 0)),
            pl.BlockSpec((1, 16, seq), lambda b, h, i: (b, 0, 0)),
            vec, vec, vec,
        ],
        out_specs=pl.BlockSpec((tq, HEAD_DIM), lambda b, h, i: (b * nq + i, h)),
        out_shape=jax.ShapeDtypeStruct((t, HEADS_W), BF16),
        scratch_shapes=[pltpu.VMEM((seq, HEAD_DIM), BF16)],
        compiler_params=_params(("arbitrary", "arbitrary", "arbitrary")),
        name="fox_attention",
    )(proj, proj, proj, proj, gates_col, gates_row,
      qn_g.reshape(1, -1), kn_g.reshape(1, -1), on_g.reshape(1, -1))


DN_HALO = 8


DN_PREP_W = 512


def _gdn_prep_kernel(q_ref, k_ref, v_ref, wq_ref, wk_ref, wv_ref, qo_ref, ko_ref, vo_ref, buf_ref, hist_ref,
                     *, tt):
    i = pl.program_id(2)

    @pl.when(i == 0)
    def _():
        hist_ref[...] = jnp.zeros_like(hist_ref)

    off = DN_HALO - (SHORT_CONV - 1)
    parts = ((q_ref, wq_ref, qo_ref), (k_ref, wk_ref, ko_ref), (v_ref, wv_ref, vo_ref))
    for part, (x_ref, w_ref, o_ref) in enumerate(parts):
        buf_ref[0:DN_HALO, :] = hist_ref[part]
        buf_ref[DN_HALO:DN_HALO + tt, :] = x_ref[...].astype(F32)
        hist_ref[part] = buf_ref[tt:tt + DN_HALO, :]
        w = w_ref[...]
        y = w[0:1, :] * buf_ref[off:off + tt, :]
        for k in range(1, SHORT_CONV):
            y = y + w[k:k + 1, :] * buf_ref[off + k:off + k + tt, :]
        y = _silu(y)
        if part < 2:
            for hh in range(DN_PREP_W // HEAD_DIM):
                yh = y[:, hh * HEAD_DIM:(hh + 1) * HEAD_DIM]
                nrm = lax.rsqrt(jnp.sum(yh * yh, axis=-1, keepdims=True) + EPS)
                if part == 0:
                    nrm = nrm * (HEAD_DIM ** -0.5)
                o_ref[:, hh * HEAD_DIM:(hh + 1) * HEAD_DIM] = (yh * nrm).astype(o_ref.dtype)
        else:
            o_ref[...] = y.astype(o_ref.dtype)


def gdn_prep(proj, dn_conv_w, bsz, seq, tt=512):
    t = proj.shape[0]
    tt = min(tt, seq)
    per_b = seq // tt
    n_c = HEADS_W // DN_PREP_W
    blk = lambda col: pl.BlockSpec((tt, DN_PREP_W), lambda b, c, i: (b * per_b + i, col // DN_PREP_W + c))
    wblk = lambda part: pl.BlockSpec((SHORT_CONV, DN_PREP_W), lambda b, c, i: (0, part * n_c + c))
    out_spec = pl.BlockSpec((tt, DN_PREP_W), lambda b, c, i: (b * per_b + i, c))
    return pl.pallas_call(
        functools.partial(_gdn_prep_kernel, tt=tt),
        grid=(bsz, n_c, per_b),
        in_specs=[blk(COL_DQ), blk(COL_DK), blk(COL_DV), wblk(0), wblk(1), wblk(2)],
        out_specs=[out_spec, out_spec, out_spec],
        out_shape=[jax.ShapeDtypeStruct((t, HEADS_W), BF16)] * 3,
        scratch_shapes=[pltpu.VMEM((tt + DN_HALO, DN_PREP_W), F32), pltpu.VMEM((3, DN_HALO, DN_PREP_W), F32)],
        compiler_params=_params(("arbitrary", "arbitrary", "arbitrary")),
        name="gdn_prep",
    )(proj, proj, proj, dn_conv_w, dn_conv_w, dn_conv_w)


GDN_GROUP = 4 * GDN_CHUNK


def _gdn_intra_kernel(q_ref, k_ref, v_ref, gcol_ref, grow_ref, u_ref, w_ref, qd_ref, kd_ref, qk_ref, gl_ref,
                      *, tt):
    h = pl.program_id(1)
    n = GDN_GROUP
    row = lax.broadcasted_iota(jnp.int32, (n, n), 0)
    col = lax.broadcasted_iota(jnp.int32, (n, n), 1)
    same = (row // GDN_CHUNK) == (col // GDN_CHUNK)
    causal = jnp.logical_and(same, row >= col)
    strict = jnp.logical_and(same, row > col)
    eye = jnp.where(row == col, 1.0, 0.0).astype(F32)
    sub = (LANE_DA + h) % 8
    nt = (((1,), (1,)), ((), ()))
    for g in range(tt // n):
        rows = slice(g * n, (g + 1) * n)
        kc = k_ref[rows, :]
        qc = q_ref[rows, :]
        vc = v_ref[rows, :].astype(F32)
        gcol = gcol_ref[rows, :]
        beta = _lane_col(gcol, LANE_DB + h)
        gcc = _lane_col(gcol, LANE_DA + h)
        gcr = grow_ref[0, pl.ds(sub, 1), rows]
        gamma = jnp.exp(jnp.where(causal, gcc - gcr, -jnp.inf))
        kf = kc.astype(F32)
        kb = kf * beta
        kk = lax.dot_general(kb.astype(BF16), kc, nt, preferred_element_type=F32)
        x = -jnp.where(strict, kk * gamma, 0.0)
        p = eye + x
        x16 = x.astype(BF16)
        x = jnp.dot(x16, x16, preferred_element_type=F32)
        for _ in range(4):
            x16 = x.astype(BF16)
            xp = jnp.dot(jnp.concatenate([x16, p.astype(BF16)], axis=0), x16, preferred_element_type=F32)
            x = xp[:n]
            p = p + xp[n:]
        p = p + jnp.dot(p.astype(BF16), x.astype(BF16), preferred_element_type=F32)
        eg = jnp.exp(gcc)
        rhs = jnp.concatenate([vc * beta, kb * eg], axis=1).astype(BF16)
        sol = jnp.dot(p.astype(BF16), rhs, preferred_element_type=F32)
        u_ref[rows, :] = sol[:, :HEAD_DIM]
        w_ref[rows, :] = sol[:, HEAD_DIM:].astype(w_ref.dtype)
        qk = jnp.where(causal, lax.dot_general(qc, kc, nt, preferred_element_type=F32) * gamma, 0.0)
        glast = jnp.concatenate(
            [jnp.broadcast_to(gcc[(c + 1) * GDN_CHUNK - 1:(c + 1) * GDN_CHUNK, :], (GDN_CHUNK, 1))
             for c in range(n // GDN_CHUNK)], axis=0)
        qd_ref[rows, :] = (qc.astype(F32) * eg).astype(qd_ref.dtype)
        kd_ref[rows, :] = (kf * jnp.exp(glast - gcc)).astype(kd_ref.dtype)
        for c in range(n // GDN_CHUNK):
            cs = slice(c * GDN_CHUNK, (c + 1) * GDN_CHUNK)
            qk_ref[0, 0, g * n + c * GDN_CHUNK:g * n + (c + 1) * GDN_CHUNK, :] = qk[cs, cs].astype(qk_ref.dtype)
            ci = g * (n // GDN_CHUNK) + c
            gl_ref[0, 0, ci:ci + 1, :] = jnp.broadcast_to(jnp.exp(glast[c * GDN_CHUNK:c * GDN_CHUNK + 1, :]),
                                                          (1, LANES))


def gdn_intra(qn, kn, vn, gates_col, gates_row, bsz, seq, tt=512):
    t = qn.shape[0]
    per_b = seq // tt
    hb = pl.BlockSpec((tt, HEAD_DIM), lambda b, h, i: (b * per_b + i, h))
    return pl.pallas_call(
        functools.partial(_gdn_intra_kernel, tt=tt),
        grid=(bsz, N_HEADS, per_b),
        in_specs=[hb, hb, hb,
                  pl.BlockSpec((tt, LANES), lambda b, h, i: (b * per_b + i, 0)),
                  pl.BlockSpec((1, 8, tt), lambda b, h, i: (b, (LANE_DA + h) // 8, i))],
        out_specs=[hb, hb, hb, hb,
                   pl.BlockSpec((1, 1, tt, GDN_CHUNK), lambda b, h, i: (b, h, i, 0)),
                   pl.BlockSpec((1, 1, tt // GDN_CHUNK, LANES), lambda b, h, i: (b, h, i, 0))],
        out_shape=[jax.ShapeDtypeStruct((t, HEADS_W), F32),
                   jax.ShapeDtypeStruct((t, HEADS_W), BF16),
                   jax.ShapeDtypeStruct((t, HEADS_W), BF16),
                   jax.ShapeDtypeStruct((t, HEADS_W), BF16),
                   jax.ShapeDtypeStruct((bsz, N_HEADS, seq, GDN_CHUNK), BF16),
                   jax.ShapeDtypeStruct((bsz, N_HEADS, seq // GDN_CHUNK, LANES), F32)],
        compiler_params=_params(("arbitrary", "arbitrary", "arbitrary")),
        name="gdn_intra",
    )(qn, kn, vn, gates_col, gates_row)


GDN_HEADS_PER_STEP = 4


def _gdn_scan_kernel(u_ref, w_ref, qd_ref, kd_ref, qk_ref, gl_ref, z_ref, og_ref, o_ref, state_ref, *, ts):
    s_idx = pl.program_id(2)

    @pl.when(s_idx == 0)
    def _():
        state_ref[...] = jnp.zeros_like(state_ref)

    tn = (((0,), (0,)), ((), ()))

    def body(c, carry):
        r0 = pl.multiple_of(c * GDN_CHUNK, GDN_CHUNK)
        for g in range(GDN_HEADS_PER_STEP):
            cols = slice(g * HEAD_DIM, (g + 1) * HEAD_DIM)
            st = state_ref[g]
            st16 = st.astype(BF16)
            v_new = u_ref[pl.ds(r0, GDN_CHUNK), cols] - jnp.dot(
                w_ref[pl.ds(r0, GDN_CHUNK), cols], st16, preferred_element_type=F32)
            vn16 = v_new.astype(BF16)
            o = jnp.dot(qd_ref[pl.ds(r0, GDN_CHUNK), cols], st16, preferred_element_type=F32)
            o = o + jnp.dot(qk_ref[0, g, pl.ds(r0, GDN_CHUNK), :], vn16, preferred_element_type=F32)
            decay = gl_ref[0, g, pl.ds(c, 1), :]
            state_ref[g] = st * decay + lax.dot_general(kd_ref[pl.ds(r0, GDN_CHUNK), cols], vn16, tn,
                                                        preferred_element_type=F32)
            oms = jnp.mean(o * o, axis=-1, keepdims=True)
            z = z_ref[pl.ds(r0, GDN_CHUNK), cols].astype(F32)
            o_ref[pl.ds(r0, GDN_CHUNK), cols] = (o * lax.rsqrt(oms + EPS) * og_ref[...] * _silu(z)).astype(o_ref.dtype)
        return carry

    lax.fori_loop(0, ts // GDN_CHUNK, body, 0)


def gdn_scan(u, w, qd, kd, qk, gl, proj, on_g, bsz, seq, ts=1024):
    t = u.shape[0]
    gw = GDN_HEADS_PER_STEP * HEAD_DIM
    ts = min(ts, seq)
    per_b = seq // ts
    n_hg = N_HEADS // GDN_HEADS_PER_STEP
    hb = pl.BlockSpec((ts, gw), lambda b, hg, s: (b * per_b + s, hg))
    return pl.pallas_call(
        functools.partial(_gdn_scan_kernel, ts=ts),
        grid=(bsz, n_hg, per_b),
        in_specs=[hb, hb, hb, hb,
                  pl.BlockSpec((1, GDN_HEADS_PER_STEP, ts, GDN_CHUNK), lambda b, hg, s: (b, hg, s, 0)),
                  pl.BlockSpec((1, GDN_HEADS_PER_STEP, ts // GDN_CHUNK, LANES), lambda b, hg, s: (b, hg, s, 0)),
                  pl.BlockSpec((ts, gw), lambda b, hg, s: (b * per_b + s, COL_DZ // gw + hg)),
                  pl.BlockSpec((1, HEAD_DIM), lambda b, hg, s: (0, 0))],
        out_specs=hb,
        out_shape=jax.ShapeDtypeStruct((t, HEADS_W), BF16),
        scratch_shapes=[pltpu.VMEM((GDN_HEADS_PER_STEP, HEAD_DIM, HEAD_DIM), F32)],
        compiler_params=_params(("arbitrary", "arbitrary", "arbitrary")),
        name="gdn_scan",
    )(u, w, qd, kd, qk, gl, proj, on_g.reshape(1, -1))


def _first_top2(vals):
    n_v = len(vals)
    m1 = vals[0]
    for v in vals[1:]:
        m1 = jnp.maximum(m1, v)
    i1 = jnp.full(m1.shape, n_v - 1, jnp.int32)
    for k in range(n_v - 2, -1, -1):
        i1 = jnp.where(vals[k] == m1, k, i1)
    rest = [jnp.where(i1 == k, -jnp.inf, vals[k]) for k in range(n_v)]
    m2 = rest[0]
    for v in rest[1:]:
        m2 = jnp.maximum(m2, v)
    i2 = jnp.full(m1.shape, n_v - 1, jnp.int32)
    for k in range(n_v - 2, -1, -1):
        i2 = jnp.where(jnp.logical_and(rest[k] == m2, i1 != k), k, i2)
    return m1, i1, m2, i2


def _routing_kernel(lg_ref, rb_ref, eidx_ref, gcol_ref, scr_ref):
    lt = lg_ref[...].T
    scores = [_sigmoid(lt[e:e + 1, :]) for e in range(N_EXPERTS)]
    sel = [scores[e] + rb_ref[e:e + 1, :] for e in range(N_EXPERTS)]
    g_score, g_i1, g_i2 = [], [], []
    for g in range(N_EXPERT_GROUPS):
        m1, i1, m2, i2 = _first_top2(sel[g * EXPERTS_PER_GROUP:(g + 1) * EXPERTS_PER_GROUP])
        g_score.append(m1 + m2)
        g_i1.append(i1)
        g_i2.append(i2)
    _, best, _, _ = _first_top2(g_score)
    l1 = g_i1[N_EXPERT_GROUPS - 1]
    l2 = g_i2[N_EXPERT_GROUPS - 1]
    for g in range(N_EXPERT_GROUPS - 2, -1, -1):
        l1 = jnp.where(best == g, g_i1[g], l1)
        l2 = jnp.where(best == g, g_i2[g], l2)
    e1 = best * EXPERTS_PER_GROUP + l1
    e2 = best * EXPERTS_PER_GROUP + l2
    s1 = jnp.zeros_like(scores[0])
    s2 = jnp.zeros_like(scores[0])
    for e in range(N_EXPERTS):
        s1 = jnp.where(e1 == e, scores[e], s1)
        s2 = jnp.where(e2 == e, scores[e], s2)
    tot = s1 + s2
    eidx_ref[...] = jnp.zeros_like(eidx_ref)
    eidx_ref[0:1, :] = e1
    eidx_ref[1:2, :] = e2
    scr_ref[...] = jnp.zeros_like(scr_ref)
    scr_ref[0:1, :] = s1 / tot
    scr_ref[1:2, :] = s2 / tot
    gcol_ref[...] = scr_ref[...].T


def moe_routing(logits, router_bias, tm=1024):
    t = logits.shape[0]
    tm = min(tm, t)
    rb = jnp.broadcast_to(jnp.zeros((LANES,), F32).at[:N_EXPERTS].set(router_bias)[:, None], (LANES, tm))
    return pl.pallas_call(
        _routing_kernel,
        grid=(t // tm,),
        in_specs=[pl.BlockSpec((tm, LANES), lambda i: (i, 0)), pl.BlockSpec((LANES, tm), lambda i: (0, 0))],
        out_specs=[pl.BlockSpec((8, tm), lambda i: (0, i)), pl.BlockSpec((tm, LANES), lambda i: (i, 0))],
        out_shape=[jax.ShapeDtypeStruct((8, t), jnp.int32), jax.ShapeDtypeStruct((t, LANES), F32)],
        scratch_shapes=[pltpu.VMEM((LANES, tm), F32)],
        compiler_params=_params(("arbitrary",)),
        name="moe_routing",
    )(logits, rb)


MOE_BLOCK = 256
POS_CHUNK = 256


def _positions_kernel(eidx_ref, dest_ref, meta_ref, *, t):
    e_full = lax.broadcasted_iota(jnp.int32, (N_EXPERTS, t), 0)
    counts = jnp.zeros((N_EXPERTS, 1), F32)
    for k in range(TOP_K):
        oh = jnp.where(e_full == eidx_ref[k:k + 1, :], 1.0, 0.0)
        counts = counts + jnp.sum(oh, axis=1, keepdims=True)
    padded = jnp.floor((counts + (MOE_BLOCK - 1)) * (1.0 / MOE_BLOCK)) * MOE_BLOCK
    e_col = lax.broadcasted_iota(jnp.int32, (N_EXPERTS, 1), 0)
    pad_start = jnp.zeros((N_EXPERTS, 1), F32)
    running = jnp.zeros((1, 1), F32)
    for e in range(N_EXPERTS):
        pad_start = jnp.where(e_col == e, running, pad_start)
        running = running + padded[e:e + 1, :]
    pad_end = pad_start + padded

    r = lax.broadcasted_iota(jnp.int32, (POS_CHUNK, POS_CHUNK), 0)
    c = lax.broadcasted_iota(jnp.int32, (POS_CHUNK, POS_CHUNK), 1)
    upper = jnp.where(r <= c, 1.0, 0.0).astype(BF16)
    e_blk_iota = lax.broadcasted_iota(jnp.int32, (N_EXPERTS, POS_CHUNK), 0)
    dest_ref[...] = jnp.zeros_like(dest_ref)
    carry = pad_start - 1.0
    for k in range(TOP_K):
        def body(j, carry, k=k):
            c0 = pl.multiple_of(j * POS_CHUNK, POS_CHUNK)
            hit = e_blk_iota == eidx_ref[k:k + 1, pl.ds(c0, POS_CHUNK)]
            pref = jnp.dot(jnp.where(hit, 1.0, 0.0).astype(BF16), upper, preferred_element_type=F32)
            pos = jnp.sum(jnp.where(hit, carry + pref, 0.0), axis=0, keepdims=True)
            dest_ref[k:k + 1, pl.ds(c0, POS_CHUNK)] = pos.astype(jnp.int32)
            return carry + pref[:, POS_CHUNK - 1:POS_CHUNK]
        carry = lax.fori_loop(0, t // POS_CHUNK, body, carry)

    blk_start = lax.broadcasted_iota(jnp.int32, (N_EXPERTS, LANES), 1).astype(F32) * MOE_BLOCK
    be = jnp.sum(jnp.where(pad_end <= blk_start, 1.0, 0.0), axis=0, keepdims=True)
    be = jnp.minimum(be, N_EXPERTS - 1.0)
    n_used = jnp.broadcast_to(running * (1.0 / MOE_BLOCK), (1, LANES))
    meta_ref[...] = jnp.zeros_like(meta_ref)
    meta_ref[0:1, :] = be.astype(jnp.int32)
    meta_ref[1:2, :] = n_used.astype(jnp.int32)


def moe_positions(eidx):
    t = eidx.shape[1]
    return pl.pallas_call(
        functools.partial(_positions_kernel, t=t),
        out_shape=[jax.ShapeDtypeStruct((8, t), jnp.int32), jax.ShapeDtypeStruct((8, LANES), jnp.int32)],
        compiler_params=pltpu.CompilerParams(vmem_limit_bytes=VMEM_LIMIT),
        name="moe_positions",
    )(eidx)


def _row_token_kernel(dest_ref, rt_ref, *, t, n_rows):
    def zero(r, carry):
        rt_ref[r] = 0
        return carry
    lax.fori_loop(0, n_rows, zero, 0, unroll=16)
    for k in range(TOP_K):
        def body(n, carry, k=k):
            rt_ref[dest_ref[k * t + n]] = n
            return carry
        lax.fori_loop(0, t, body, 0, unroll=16)


def moe_row_token(dest_flat, t, n_rows):
    return pl.pallas_call(
        functools.partial(_row_token_kernel, t=t, n_rows=n_rows),
        in_specs=[pl.BlockSpec(memory_space=pltpu.SMEM)],
        out_specs=pl.BlockSpec(memory_space=pltpu.SMEM),
        out_shape=jax.ShapeDtypeStruct((n_rows,), jnp.int32),
        name="moe_row_token",
    )(dest_flat)


def _gather_rows_kernel(rt_ref, meta_ref, h_ref, o_ref, buf_ref, sem):
    blk = pl.program_id(0)

    @pl.when(blk < meta_ref[LANES])
    def _():
        base = blk * MOE_BLOCK

        def row_copy(r):
            tok = rt_ref[base + r]
            return pltpu.make_async_copy(h_ref.at[pl.ds(tok, 1), :], buf_ref.at[pl.ds(r, 1), :], sem)

        def start(r, carry):
            row_copy(r).start()
            return carry
        lax.fori_loop(0, MOE_BLOCK, start, 0, unroll=8)
        pltpu.make_async_copy(h_ref.at[pl.ds(0, MOE_BLOCK), :], buf_ref, sem).wait()
        o_ref[...] = buf_ref[...].astype(o_ref.dtype)

    @pl.when(blk >= meta_ref[LANES])
    def _():
        o_ref[...] = jnp.zeros_like(o_ref)


def moe_gather_rows(row_token, meta_flat, h, n_rows):
    d = h.shape[1]
    n_blocks = n_rows // MOE_BLOCK
    grid_spec = pltpu.PrefetchScalarGridSpec(
        num_scalar_prefetch=2,
        grid=(n_blocks,),
        in_specs=[pl.BlockSpec(memory_space=pl.ANY)],
        out_specs=pl.BlockSpec((MOE_BLOCK, d), lambda i, rt, meta: (i, 0)),
        scratch_shapes=[pltpu.VMEM((MOE_BLOCK, d), F32), pltpu.SemaphoreType.DMA],
    )
    return pl.pallas_call(
        _gather_rows_kernel,
        grid_spec=grid_spec,
        out_shape=jax.ShapeDtypeStruct((n_rows, d), BF16),
        compiler_params=_params(("arbitrary",)),
        name="moe_gather_rows",
    )(row_token, meta_flat, h)


def _expert_changed(meta_ref, blk):
    prev = meta_ref[jnp.maximum(blk - 1, 0)]
    return jnp.logical_or(blk == 0, meta_ref[blk] != prev)


def _moe_up_kernel(meta_ref, x_ref, wg_ref, wu_ref, o_ref, wg16_ref, wu16_ref):
    blk = pl.program_id(1)

    @pl.when(blk < meta_ref[LANES])
    def _():
        @pl.when(_expert_changed(meta_ref, blk))
        def _():
            _cast_weight_tile(wg_ref.at[0, 0], wg16_ref)
            _cast_weight_tile(wu_ref.at[0, 0], wu16_ref)
        x = x_ref[...]
        a = jnp.dot(x, wg16_ref[...], preferred_element_type=F32)
        b = jnp.dot(x, wu16_ref[...], preferred_element_type=F32)
        o_ref[...] = (_silu(a) * b).astype(o_ref.dtype)

    @pl.when(blk >= meta_ref[LANES])
    def _():
        o_ref[...] = jnp.zeros_like(o_ref)


def _moe_down_kernel(meta_ref, h_ref, wd_ref, o_ref, wd16_ref):
    blk = pl.program_id(1)

    @pl.when(blk < meta_ref[LANES])
    def _():
        @pl.when(_expert_changed(meta_ref, blk))
        def _():
            _cast_weight_tile(wd_ref.at[0, 0], wd16_ref)
        o_ref[...] = jnp.dot(h_ref[...], wd16_ref[...], preferred_element_type=F32).astype(o_ref.dtype)

    @pl.when(blk >= meta_ref[LANES])
    def _():
        o_ref[...] = jnp.zeros_like(o_ref)


def _used(i, meta):
    return jnp.minimum(i, meta[LANES] - 1)


def moe_up(meta_flat, xs, w_gate, w_up, layer, tn=512):
    n_rows, d = xs.shape
    de = w_gate.shape[3]
    n_blocks = n_rows // MOE_BLOCK
    wspec = pl.BlockSpec((1, 1, d, tn), lambda c, i, meta: (layer, meta[_used(i, meta)], 0, c))
    grid_spec = pltpu.PrefetchScalarGridSpec(
        num_scalar_prefetch=1,
        grid=(de // tn, n_blocks),
        in_specs=[pl.BlockSpec((MOE_BLOCK, d), lambda c, i, meta: (_used(i, meta), 0)), wspec, wspec],
        out_specs=pl.BlockSpec((MOE_BLOCK, tn), lambda c, i, meta: (i, c)),
        scratch_shapes=[pltpu.VMEM((d, tn), BF16), pltpu.VMEM((d, tn), BF16)],
    )
    return pl.pallas_call(
        _moe_up_kernel,
        grid_spec=grid_spec,
        out_shape=jax.ShapeDtypeStruct((n_rows, de), BF16),
        compiler_params=_params(("arbitrary", "arbitrary")),
        name="moe_up",
    )(meta_flat, xs, w_gate, w_up)


def moe_down(meta_flat, hs, w_down, layer, tn=2048):
    n_rows, de = hs.shape
    d = w_down.shape[3]
    n_blocks = n_rows // MOE_BLOCK
    grid_spec = pltpu.PrefetchScalarGridSpec(
        num_scalar_prefetch=1,
        grid=(d // tn, n_blocks),
        in_specs=[pl.BlockSpec((MOE_BLOCK, de), lambda c, i, meta: (_used(i, meta), 0)),
                  pl.BlockSpec((1, 1, de, tn), lambda c, i, meta: (layer, meta[_used(i, meta)], 0, c))],
        out_specs=pl.BlockSpec((MOE_BLOCK, tn), lambda c, i, meta: (i, c)),
        scratch_shapes=[pltpu.VMEM((de, tn), BF16)],
    )
    return pl.pallas_call(
        _moe_down_kernel,
        grid_spec=grid_spec,
        out_shape=jax.ShapeDtypeStruct((n_rows, d), F32),
        compiler_params=_params(("arbitrary", "arbitrary")),
        name="moe_down",
    )(meta_flat, hs, w_down)


def _combine_kernel(dest_ref, rows_ref, x_ref, gcol_ref, gate_ref, fg_ref, o_ref, buf_ref, sem, *, tm, t, final):
    i = pl.program_id(0)
    base = i * tm

    def row_copy(k, r):
        src = dest_ref[k * t + base + r]
        return pltpu.make_async_copy(rows_ref.at[pl.ds(src, 1), :], buf_ref.at[k, pl.ds(r, 1), :], sem)

    def start(r, carry):
        for k in range(TOP_K):
            row_copy(k, r).start()
        return carry
    lax.fori_loop(0, tm, start, 0, unroll=4)
    for k in range(TOP_K):
        pltpu.make_async_copy(rows_ref.at[pl.ds(0, tm), :], buf_ref.at[k], sem).wait()

    g = gcol_ref[...]
    y = _lane_col(g, 0) * buf_ref[0] + _lane_col(g, 1) * buf_ref[1]
    xn = x_ref[...] + gate_ref[0] * y
    if final:
        ms = jnp.mean(xn * xn, axis=-1, keepdims=True)
        xn = xn * lax.rsqrt(ms + EPS) * fg_ref[...]
    o_ref[...] = xn


def moe_combine(dest_flat, out_rows, x2, gates_col, gate_f, final_g, seq, final, tm=256):
    t, d = x2.shape
    bsz = gate_f.shape[0]
    per_b = seq // tm
    grid_spec = pltpu.PrefetchScalarGridSpec(
        num_scalar_prefetch=1,
        grid=(t // tm,),
        in_specs=[pl.BlockSpec(memory_space=pl.ANY),
                  pl.BlockSpec((tm, d), lambda i, dest: (i, 0)),
                  pl.BlockSpec((tm, LANES), lambda i, dest: (i, 0)),
                  pl.BlockSpec((1, 1, d), lambda i, dest: (i // per_b, 0, 0)),
                  pl.BlockSpec((1, d), lambda i, dest: (0, 0))],
        out_specs=pl.BlockSpec((tm, d), lambda i, dest: (i, 0)),
        scratch_shapes=[pltpu.VMEM((TOP_K, tm, d), F32), pltpu.SemaphoreType.DMA],
    )
    return pl.pallas_call(
        functools.partial(_combine_kernel, tm=tm, t=t, final=final),
        grid_spec=grid_spec,
        out_shape=jax.ShapeDtypeStruct((t, d), F32),
        compiler_params=_params(("arbitrary",)),
        name="moe_combine",
    )(dest_flat, out_rows, x2, gates_col, gate_f.reshape(bsz, 1, d), final_g.reshape(1, d))


def moe_ffn(h32, logits, router_bias, w_gate, w_up, w_down, layer, x2, gate_f, final_g, seq, final):
    t = h32.shape[0]
    n_rows = t * TOP_K + N_EXPERTS * MOE_BLOCK
    eidx, gates_col = moe_routing(logits, router_bias)
    dest, meta = moe_positions(eidx)
    dest_flat = dest[:TOP_K].reshape(TOP_K * t)
    meta_flat = meta[:2].reshape(2 * LANES)
    row_token = moe_row_token(dest_flat, t, n_rows)
    xs = moe_gather_rows(row_token, meta_flat, h32, n_rows)
    hs = moe_up(meta_flat, xs, w_gate, w_up, layer)
    out_rows = moe_down(meta_flat, hs, w_down, layer)
    return moe_combine(dest_flat, out_rows, x2, gates_col, gate_f, final_g, seq, final)


def kernel(x, c, ada_w, ada_b, norm_mix_g, norm_ffn_g, w_in, conv_w, conv_b, conv_ln_g, conv_ln_b,
           fox_f_bias, fox_qn_g, fox_kn_g, fox_on_g, dn_conv_w, dn_a_log, dn_dt_bias, dn_on_g, w_out,
           w_router, router_bias, w_gate_e, w_up_e, w_down_e, final_g):
    bsz, seq, d = x.shape
    depth = ada_w.shape[0]
    t = bsz * seq
    x2 = x.reshape(t, d)
    mod = adaln(c, ada_w, ada_b)
    for l in range(depth):
        shift_m, scale_m, gate_m, shift_f, scale_f, gate_f = [mod[l, :, k * d:(k + 1) * d] for k in range(6)]
        h = norm_mod(x2, norm_mix_g[l], scale_m, shift_m, seq)
        proj = in_proj(h, w_in, l)
        gates_col, gates_row = gates_prep(h, w_in, l, fox_f_bias[l], dn_a_log[l], dn_dt_bias[l], bsz, seq)
        y_a = conv_module(proj, conv_w[l], conv_b[l], conv_ln_g[l], conv_ln_b[l], bsz, seq)
        y_b = fox_attention(proj, gates_col, gates_row, fox_qn_g[l], fox_kn_g[l], fox_on_g[l], bsz, seq)
        qn, kn, vn = gdn_prep(proj, dn_conv_w[l], bsz, seq)
        u, w, qd, kd, qk, gl = gdn_intra(qn, kn, vn, gates_col, gates_row, bsz, seq)
        y_c = gdn_scan(u, w, qd, kd, qk, gl, proj, dn_on_g[l], bsz, seq)
        x2 = out_proj(y_a, y_b, y_c, w_out, l, x2, gate_m, seq)
        h32, logits = norm_mod(x2, norm_ffn_g[l], scale_f, shift_f, seq, w_router=w_router)
        x2 = moe_ffn(h32, logits, router_bias, w_gate_e, w_up_e, w_down_e, l, x2, gate_f, final_g,
                     seq, final=(l == depth - 1))
    return x2.reshape(bsz, seq, d)
```
